```python
import jax, jax.numpy as jnp
from jax import lax
import numpy as np

D_MODEL = 1024
BATCH = 4
SEQ = 4096
DEPTH = 2

D_MIX = D_MODEL
D_A = D_MIX // 2
D_B = D_MIX - D_A
A_GROUPS = 4
A_GROUP_DIM = D_A // A_GROUPS
A_CHUNK = 128
B_HEADS = 4
B_HEAD_DIM = D_B // B_HEADS
B_CHUNK = 64
D_IN = 2 * D_A + 4 * D_B
D_FF = 4 * D_MODEL
EPS = 1e-6

kernel_name = "hybrid_gmlp_hgrn2_parallel_heads"


def rms_norm(x, w):
    x32 = x.astype(jnp.float32)
    y = x32 * lax.rsqrt(jnp.mean(jnp.square(x32), axis=-1, keepdims=True) + EPS)
    return (y * w.astype(jnp.float32)).astype(x.dtype)


def group_rms_norm(x, w, groups):
    lead = x.shape[:-1]
    x32 = x.astype(jnp.float32).reshape(*lead, groups, -1)
    y = x32 * lax.rsqrt(jnp.mean(jnp.square(x32), axis=-1, keepdims=True) + EPS)
    y = y.reshape(*lead, -1) * w.astype(jnp.float32)
    return y.astype(x.dtype)


def gmlp_chunk_mixer(u, v, w_s, b_s, g_v, g_out):
    bsz, seq, _ = u.shape
    n_chunks = seq // A_CHUNK
    v = group_rms_norm(v, g_v, A_GROUPS)
    v = v.reshape(bsz, n_chunks, A_CHUNK, A_GROUPS, A_GROUP_DIM)
    w_causal = w_s * jnp.tril(jnp.ones((A_CHUNK, A_CHUNK), w_s.dtype))
    mixed = jnp.einsum('gts,bnsgc->bntgc', w_causal, v) + b_s.T[:, :, None]
    y = u * mixed.reshape(bsz, seq, D_A)
    return group_rms_norm(y, g_out, A_GROUPS)


def hgrn2_chunkwise(q, k, v, log_f):
    bsz, seq, heads, dk = q.shape
    dv = v.shape[-1]
    n_chunks = seq // B_CHUNK

    def to_chunks(t):
        return t.reshape(bsz, n_chunks, B_CHUNK, heads, t.shape[-1]).transpose(1, 0, 3, 2, 4)

    qc, kc, vc, gc = (to_chunks(t) for t in (q, k, v, log_f))
    bc = jnp.cumsum(gc, axis=-2)
    causal = jnp.tril(jnp.ones((B_CHUNK, B_CHUNK), bool))[:, :, None]

    def step(state, inp):
        qb, kb, vb, bb = inp
        diff = bb[:, :, :, None, :] - bb[:, :, None, :, :]
        decay = jnp.exp(jnp.where(causal, diff, -jnp.inf))
        scores = jnp.einsum('bhtk,bhsk,bhtsk->bhts', qb, kb, decay)
        o_intra = jnp.einsum('bhts,bhsv->bhtv', scores, vb)
        o_inter = jnp.einsum('bhtk,bhkv->bhtv', qb * jnp.exp(bb), state)
        b_last = bb[:, :, -1, :]
        new_state = state * jnp.exp(b_last)[..., None] + jnp.einsum(
            'bhsk,bhsv->bhkv', kb * jnp.exp(b_last[:, :, None, :] - bb), vb)
        return new_state, o_intra + o_inter

    state0 = jnp.zeros((bsz, heads, dk, dv), jnp.float32)
    _, out = lax.scan(step, state0, (qc, kc, vc, bc))
    return out.transpose(1, 0, 3, 2, 4).reshape(bsz, seq, heads * dv)


def hybrid_mixer(h, w_in, w_s, b_s, g_v, g_a_out, lb, g_b_out, w_out):
    bsz, seq, _ = h.shape
    z = h @ w_in
    za_u, za_v, zq, zf, zi, zg = jnp.split(
        z, [D_A, 2 * D_A, 2 * D_A + D_B, 2 * D_A + 2 * D_B, 2 * D_A + 3 * D_B], axis=-1)

    y_a = gmlp_chunk_mixer(jax.nn.gelu(za_u), jax.nn.gelu(za_v), w_s, b_s, g_v, g_a_out)

    z32 = zf.astype(jnp.float32)
    lb32 = lb.astype(jnp.float32)
    log_f = jnp.logaddexp(jnp.log(lb32), jnp.log1p(-lb32) + jax.nn.log_sigmoid(z32))
    k = (1.0 - lb32) * jax.nn.sigmoid(-z32)
    shp = (bsz, seq, B_HEADS, B_HEAD_DIM)
    o_b = hgrn2_chunkwise(zq.astype(jnp.float32).reshape(shp), k.reshape(shp),
                          zi.astype(jnp.float32).reshape(shp), log_f.reshape(shp))
    y_b = group_rms_norm(o_b.astype(h.dtype), g_b_out, B_HEADS) * jax.nn.silu(zg)

    return jnp.concatenate([y_a, y_b], axis=-1) @ w_out


def sqrelu_mlp(h, w_up, w_down):
    return jnp.square(jax.nn.relu(h @ w_up)) @ w_down


def setup_inputs(seed: int = 0) -> dict:
    key = jax.random.key(seed)
    ks = jax.random.split(key, 16)
    nrm = jax.random.normal
    f32 = jnp.float32
    return {
        "x": nrm(ks[0], (BATCH, SEQ, D_MODEL), f32),
        "norm_mix": 1.0 + 0.02 * nrm(ks[1], (DEPTH, D_MODEL), f32),
        "w_in": nrm(ks[2], (DEPTH, D_MODEL, D_IN), f32) * D_MODEL ** -0.5,
        "spatial_w": nrm(ks[3], (DEPTH, A_GROUPS, A_CHUNK, A_CHUNK), f32) * A_CHUNK ** -0.5,
        "spatial_b": 1.0 + 0.1 * nrm(ks[4], (DEPTH, A_GROUPS, A_CHUNK), f32),
        "norm_v": 1.0 + 0.02 * nrm(ks[5], (DEPTH, D_A), f32),
        "norm_a_out": 1.0 + 0.02 * nrm(ks[6], (DEPTH, D_A), f32),
        "lower_bounds": 0.5 * nrm(ks[7], (DEPTH, D_B), f32),
        "norm_b_out": 1.0 + 0.02 * nrm(ks[8], (DEPTH, D_B), f32),
        "w_out": nrm(ks[9], (DEPTH, D_MIX, D_MODEL), f32) * D_MIX ** -0.5,
        "norm_mlp": 1.0 + 0.02 * nrm(ks[10], (DEPTH, D_MODEL), f32),
        "w_up": nrm(ks[11], (DEPTH, D_MODEL, D_FF), f32) * D_MODEL ** -0.5,
        "w_down": nrm(ks[12], (DEPTH, D_FF, D_MODEL), f32) * D_FF ** -0.5,
        "norm_final": 1.0 + 0.02 * nrm(ks[13], (D_MODEL,), f32),
    }


def reference(x, norm_mix, w_in, spatial_w, spatial_b, norm_v, norm_a_out, lower_bounds,
              norm_b_out, w_out, norm_mlp, w_up, w_down, norm_final):
    lb_all = jnp.cumsum(jax.nn.softmax(lower_bounds.astype(jnp.float32), axis=0), axis=0)
    lb_all = lb_all - lb_all[0:1]
    for l in range(DEPTH):
        h = rms_norm(x, norm_mix[l])
        x = x + hybrid_mixer(h, w_in[l], spatial_w[l], spatial_b[l], norm_v[l], norm_a_out[l],
                             lb_all[l], norm_b_out[l], w_out[l])
        h = rms_norm(x, norm_mlp[l])
        x = x + sqrelu_mlp(h, w_up[l], w_down[l])
    return rms_norm(x, norm_final)
```

```python
import functools

import jax
import jax.numpy as jnp
from jax import lax
from jax.experimental import pallas as pl
from jax.experimental.pallas import tpu as pltpu

D_MODEL = 1024
D_A = 512
D_B = 512
A_GROUPS = 4
A_CHUNK = 128
B_HEADS = 4
HEAD = 128
B_CHUNK = 64
D_IN = 2 * D_A + 4 * D_B
D_FF = 4 * D_MODEL
EPS = 1e-6

SUBLANES = 8
N_LEVELS = 6
VMEM_LIMIT_BYTES = 56 * 1024 * 1024

MIX_ROWS = 256
MLP_ROWS = 512
FF_BLOCK = 1024

BF16 = jnp.bfloat16
F32 = jnp.float32


def _dot(a, b):
    return jnp.dot(a, b, preferred_element_type=F32)


def _dot_nt(a, b):
    return lax.dot_general(a, b, (((1,), (1,)), ((), ())), preferred_element_type=F32)


def _dot_tn(a, b):
    return lax.dot_general(a, b, (((0,), (0,)), ((), ())), preferred_element_type=F32)


def _rms(x, w):
    return x * lax.rsqrt(jnp.mean(x * x, axis=-1, keepdims=True) + EPS) * w


def _group_rms(x, w, groups):
    width = x.shape[-1] // groups
    parts = []
    for g in range(groups):
        sl = slice(g * width, (g + 1) * width)
        parts.append(_rms(x[:, sl], w[:, sl]))
    return jnp.concatenate(parts, axis=-1)


def _mid_rows(b, half):
    rows, width = b.shape
    n = 2 * half
    if half >= SUBLANES // 2:
        b3 = b.reshape(rows // n, n, width)
        mid = jnp.broadcast_to(b3[:, half - 1:half, :], b3.shape)
        return mid.reshape(rows, width)
    b3 = b.reshape(rows // SUBLANES, SUBLANES, width)
    pos = lax.broadcasted_iota(jnp.int32, b3.shape, 1) % n
    prev1 = pltpu.roll(b3, 1, 1)
    if half == 1:
        mid = jnp.where(pos == 1, prev1, b3)
    else:
        nxt1 = pltpu.roll(b3, SUBLANES - 1, 1)
        prev2 = pltpu.roll(b3, 2, 1)
        mid = jnp.where(pos == 0, nxt1, jnp.where(pos == 1, b3, jnp.where(pos == 2, prev1, prev2)))
    return mid.reshape(rows, width)


def _mixer_kernel(layer, x_ref, nmix_ref, win_ref, sw_ref, sbt_ref, nv_ref, nao_ref, lbs_ref,
                  nbo_ref, wout_ref, o_ref,
                  st_ref, z_ref, qs_ref, ks_ref, qe_ref, ke_ref, v_ref, eb_ref, ob_ref, y_ref):
    rows = x_ref.shape[0]

    @pl.when(pl.program_id(1) == 0)
    def _():
        st_ref[...] = jnp.zeros_like(st_ref)

    x = x_ref[...]
    h = _rms(x, nmix_ref[...])
    z_ref[...] = _dot(h.astype(BF16), win_ref[...])

    u = jax.nn.gelu(z_ref[:, 0:D_A])
    v = _group_rms(jax.nn.gelu(z_ref[:, D_A:2 * D_A]), nv_ref[...], A_GROUPS).astype(BF16)
    tri = (lax.broadcasted_iota(jnp.int32, (A_CHUNK, A_CHUNK), 0)
           >= lax.broadcasted_iota(jnp.int32, (A_CHUNK, A_CHUNK), 1))
    ya_parts = []
    for g in range(A_GROUPS):
        w_g = jnp.where(tri, sw_ref[g], 0.0).astype(BF16)
        bias_g = sbt_ref[:, g:g + 1]
        sl = slice(g * HEAD, (g + 1) * HEAD)
        blocks = []
        for c in range(rows // A_CHUNK):
            rs = slice(c * A_CHUNK, (c + 1) * A_CHUNK)
            mixed = _dot(w_g, v[rs, sl]) + bias_g
            blocks.append(u[rs, sl] * mixed)
        y_g = jnp.concatenate(blocks, axis=0)
        ya_parts.append(_rms(y_g, nao_ref[:, sl]))
    y_ref[:, 0:D_A] = jnp.concatenate(ya_parts, axis=-1).astype(BF16)

    base = 2 * D_A
    zq = z_ref[:, base:base + D_B]
    zf = z_ref[:, base + D_B:base + 2 * D_B]
    v_ref[...] = z_ref[:, base + 2 * D_B:base + 3 * D_B].astype(BF16)

    lbs = lbs_ref[...]
    sm = jnp.exp(lbs - jnp.max(lbs, axis=0, keepdims=True))
    sm = sm / jnp.sum(sm, axis=0, keepdims=True)
    lb = jnp.sum(sm[1:layer + 1, :], axis=0, keepdims=True) if layer > 0 else jnp.zeros((1, D_B), F32)

    soft = jnp.log1p(jnp.exp(-jnp.abs(zf)))
    log_sig = jnp.minimum(zf, 0.0) - soft
    log_sig_neg = jnp.minimum(-zf, 0.0) - soft
    t0 = jnp.log(lb)
    t1 = jnp.log1p(-lb) + log_sig
    log_f = jnp.maximum(t0, t1) + jnp.log1p(jnp.exp(-jnp.abs(t0 - t1)))
    kk = (1.0 - lb) * jnp.exp(log_sig_neg)

    g_hi = log_f.astype(BF16)
    r1 = log_f - g_hi.astype(F32)
    g_mid = r1.astype(BF16)
    g_lo = (r1 - g_mid.astype(F32)).astype(BF16)
    ltri = (lax.broadcasted_iota(jnp.int32, (B_CHUNK, B_CHUNK), 0)
            >= lax.broadcasted_iota(jnp.int32, (B_CHUNK, B_CHUNK), 1)).astype(BF16)
    b_parts = []
    for c in range(rows // B_CHUNK):
        rs = slice(c * B_CHUNK, (c + 1) * B_CHUNK)
        b_parts.append(_dot(ltri, g_hi[rs]) + _dot(ltri, g_mid[rs]) + _dot(ltri, g_lo[rs]))
    b = jnp.concatenate(b_parts, axis=0)

    for lvl in range(N_LEVELS):
        e = jnp.exp(-jnp.abs(b - _mid_rows(b, 1 << lvl)))
        qs_ref[lvl] = (zq * e).astype(BF16)
        ks_ref[lvl] = (kk * e).astype(BF16)
    qs_ref[N_LEVELS] = zq.astype(BF16)
    ks_ref[N_LEVELS] = kk.astype(BF16)

    b3 = b.reshape(rows // B_CHUNK, B_CHUNK, D_B)
    b_last = b3[:, B_CHUNK - 1:B_CHUNK, :]
    qe_ref[...] = (zq * jnp.exp(b)).astype(BF16)
    ke_ref[...] = (kk.reshape(b3.shape) * jnp.exp(b_last - b3)).reshape(rows, D_B).astype(BF16)
    eb_ref[...] = jnp.exp(b_last).reshape(rows // B_CHUNK, D_B)

    ri = lax.broadcasted_iota(jnp.int32, (B_CHUNK, B_CHUNK), 0)
    ci = lax.broadcasted_iota(jnp.int32, (B_CHUNK, B_CHUNK), 1)
    xor = ri ^ ci
    masks = [(ri > ci) & (xor >= (1 << lvl)) & (xor < (2 << lvl)) for lvl in range(N_LEVELS)]
    masks.append(ri == ci)

    def chunk_body(c, carry):
        r0 = pl.multiple_of(c * B_CHUNK, B_CHUNK)
        rs = pl.ds(r0, B_CHUNK)
        decay = eb_ref[pl.ds(c, 1), :]
        for hd in range(B_HEADS):
            sl = slice(hd * HEAD, (hd + 1) * HEAD)
            scores = jnp.zeros((B_CHUNK, B_CHUNK), F32)
            for lvl in range(N_LEVELS + 1):
                p = _dot_nt(qs_ref[lvl, rs, sl], ks_ref[lvl, rs, sl])
                scores = jnp.where(masks[lvl], p, scores)
            vh = v_ref[rs, sl]
            st = st_ref[hd]
            o = _dot(scores.astype(BF16), vh) + _dot_nt(qe_ref[rs, sl], st.astype(BF16))
            ob_ref[rs, sl] = o
            st_ref[hd] = st * decay[:, sl] + _dot_tn(vh, ke_ref[rs, sl])
        return carry

    lax.fori_loop(0, rows // B_CHUNK, chunk_body, 0)

    zg = z_ref[:, base + 3 * D_B:base + 4 * D_B]
    gate = zg / (1.0 + jnp.exp(-zg))
    y_ref[:, D_A:] = (_group_rms(ob_ref[...], nbo_ref[...], B_HEADS) * gate).astype(BF16)

    o_ref[...] = x + _dot(y_ref[...], wout_ref[...])


def _mlp_kernel(final, x_ref, nw_ref, wup_ref, wdn_ref, nf_ref, o_ref):
    x = x_ref[...]
    h = _rms(x, nw_ref[...]).astype(BF16)
    acc = x
    for j in range(D_FF // FF_BLOCK):
        cs = slice(j * FF_BLOCK, (j + 1) * FF_BLOCK)
        a = jnp.maximum(_dot(h, wup_ref[:, cs]), 0.0)
        acc = acc + _dot((a * a).astype(BF16), wdn_ref[cs, :])
    if final:
        acc = _rms(acc, nf_ref[...])
    o_ref[...] = acc


def _full(shape):
    return pl.BlockSpec(shape, lambda *_: (0,) * len(shape))


def _mixer(layer, x, nmix, win, sw, sbt, nv, nao, lbs, nbo, wout):
    bsz, seq, _ = x.shape
    rows = MIX_ROWS
    n_chunks = rows // B_CHUNK
    xspec = pl.BlockSpec((None, rows, D_MODEL), lambda bi, si: (bi, si, 0))
    return pl.pallas_call(
        functools.partial(_mixer_kernel, layer),
        grid=(bsz, seq // rows),
        in_specs=[xspec, _full(nmix.shape), _full(win.shape), _full(sw.shape), _full(sbt.shape),
                  _full(nv.shape), _full(nao.shape), _full(lbs.shape), _full(nbo.shape),
                  _full(wout.shape)],
        out_specs=xspec,
        out_shape=jax.ShapeDtypeStruct(x.shape, F32),
        scratch_shapes=[
            pltpu.VMEM((B_HEADS, HEAD, HEAD), F32),
            pltpu.VMEM((rows, D_IN), F32),
            pltpu.VMEM((N_LEVELS + 1, rows, D_B), BF16),
            pltpu.VMEM((N_LEVELS + 1, rows, D_B), BF16),
            pltpu.VMEM((rows, D_B), BF16),
            pltpu.VMEM((rows, D_B), BF16),
            pltpu.VMEM((rows, D_B), BF16),
            pltpu.VMEM((n_chunks, D_B), F32),
            pltpu.VMEM((rows, D_B), F32),
            pltpu.VMEM((rows, D_MODEL), BF16),
        ],
        compiler_params=pltpu.CompilerParams(
            dimension_semantics=("arbitrary", "arbitrary"), vmem_limit_bytes=VMEM_LIMIT_BYTES),
        name=f"mixer{layer}",
    )(x, nmix, win, sw, sbt, nv, nao, lbs, nbo, wout)


def _mlp(layer, final, x2, nw, wup, wdn, nf):
    n_tok = x2.shape[0]
    xspec = pl.BlockSpec((MLP_ROWS, D_MODEL), lambda i: (i, 0))
    return pl.pallas_call(
        functools.partial(_mlp_kernel, final),
        grid=(n_tok // MLP_ROWS,),
        in_specs=[xspec, _full(nw.shape), _full(wup.shape), _full(wdn.shape), _full(nf.shape)],
        out_specs=xspec,
        out_shape=jax.ShapeDtypeStruct(x2.shape, F32),
        compiler_params=pltpu.CompilerParams(
            dimension_semantics=("arbitrary",), vmem_limit_bytes=VMEM_LIMIT_BYTES),
        name=f"mlp{layer}",
    )(x2, nw, wup, wdn, nf)


def kernel(x, norm_mix, w_in, spatial_w, spatial_b, norm_v, norm_a_out, lower_bounds, norm_b_out,
           w_out, norm_mlp, w_up, w_down, norm_final):
    bsz, seq, d = x.shape
    depth = w_in.shape[0]
    assert d == D_MODEL and seq % MIX_ROWS == 0 and (bsz * seq) % MLP_ROWS == 0
    row = lambda a: a.reshape(1, -1).astype(F32)
    for l in range(depth):
        x = _mixer(l, x, row(norm_mix[l]), w_in[l].astype(BF16), spatial_w[l], spatial_b[l].T,
                   row(norm_v[l]), row(norm_a_out[l]), lower_bounds.astype(F32), row(norm_b_out[l]),
                   w_out[l].astype(BF16))
        x2 = _mlp(l, l == depth - 1, x.reshape(bsz * seq, d), row(norm_mlp[l]),
                  w_up[l].astype(BF16), w_down[l].astype(BF16), row(norm_final))
        x = x2.reshape(bsz, seq, d)
    return x
```

```python
import functools

import jax
import jax.numpy as jnp
from jax import lax
from jax.experimental import pallas as pl
from jax.experimental.pallas import tpu as pltpu

D_MODEL = 1024
D_A = 512
D_B = 512
A_GROUPS = 4
A_CHUNK = 128
B_HEADS = 4
HEAD = 128
B_CHUNK = 64
D_IN = 2 * D_A + 4 * D_B
D_FF = 4 * D_MODEL
EPS = 1e-6
LOG2E = 1.4426950408889634
SQRT_2_OVER_PI = 0.7978845608028654

SUBLANES = 8
N_LEVELS = 6
VMEM_LIMIT_BYTES = 56 * 1024 * 1024

MIX_ROWS = 256
MLP_ROWS = 512
FF_BLOCK = 1024

BF16 = jnp.bfloat16
F32 = jnp.float32


def _dot(a, b):
    return jnp.dot(a, b, preferred_element_type=F32)


def _dot_nt(a, b):
    return lax.dot_general(a, b, (((1,), (1,)), ((), ())), preferred_element_type=F32)


def _dot_tn(a, b):
    return lax.dot_general(a, b, (((0,), (0,)), ((), ())), preferred_element_type=F32)


def _rms(x, w):
    return x * lax.rsqrt(jnp.mean(x * x, axis=-1, keepdims=True) + EPS) * w


def _group_rms(x, w, groups):
    width = x.shape[-1] // groups
    parts = []
    for g in range(groups):
        sl = slice(g * width, (g + 1) * width)
        parts.append(_rms(x[:, sl], w[:, sl]))
    return jnp.concatenate(parts, axis=-1)


def _gelu(x):
    a = -2.0 * SQRT_2_OVER_PI * LOG2E
    return x / (1.0 + jnp.exp2(x * (a + (a * 0.044715) * (x * x))))


def _level_decay(b, half):
    rows, width = b.shape
    n = 2 * half
    if half >= SUBLANES:
        b4 = b.reshape(rows // n, 2, half, width)
        first, second = b4[:, 0], b4[:, 1]
        mid = first[:, half - 1:half, :]
        e = jnp.stack([jnp.exp2(mid - first), jnp.exp2(second - mid)], axis=1)
        return e.reshape(rows, width)
    b3 = b.reshape(rows // SUBLANES, SUBLANES, width)
    if half == SUBLANES // 2:
        mid = jnp.broadcast_to(b3[:, half - 1:half, :], b3.shape)
    else:
        assert half == 2
        pos = lax.broadcasted_iota(jnp.int32, b3.shape, 1) % n
        prev1 = pltpu.roll(b3, 1, 1)
        nxt1 = pltpu.roll(b3, SUBLANES - 1, 1)
        prev2 = pltpu.roll(b3, 2, 1)
        mid = jnp.where(pos == 0, nxt1, jnp.where(pos == 1, b3, jnp.where(pos == 2, prev1, prev2)))
    d = b3 - mid
    return jnp.exp2(jnp.minimum(d, -d)).reshape(rows, width)


def _mixer_kernel(layer, x_ref, nmix_ref, win_ref, sw_ref, sbt_ref, nv_ref, nao_ref, lbs_ref,
                  nbo_ref, wout_ref, o_ref,
                  st_ref, z_ref, qs_ref, ks_ref, qe_ref, ke_ref, v_ref, eb_ref, ob_ref, y_ref):
    rows = x_ref.shape[0]

    @pl.when(pl.program_id(1) == 0)
    def _():
        st_ref[...] = jnp.zeros_like(st_ref)

    x = x_ref[...]
    h = _rms(x, nmix_ref[...])
    z_ref[...] = _dot(h.astype(BF16), win_ref[...])

    u = _gelu(z_ref[:, 0:D_A])
    v = _group_rms(_gelu(z_ref[:, D_A:2 * D_A]), nv_ref[...], A_GROUPS).astype(BF16)
    tri = (lax.broadcasted_iota(jnp.int32, (A_CHUNK, A_CHUNK), 0)
           >= lax.broadcasted_iota(jnp.int32, (A_CHUNK, A_CHUNK), 1))
    ya_parts = []
    for g in range(A_GROUPS):
        w_g = jnp.where(tri, sw_ref[g], 0.0).astype(BF16)
        bias_g = sbt_ref[:, g:g + 1]
        sl = slice(g * HEAD, (g + 1) * HEAD)
        blocks = []
        for c in range(rows // A_CHUNK):
            rs = slice(c * A_CHUNK, (c + 1) * A_CHUNK)
            mixed = _dot(w_g, v[rs, sl]) + bias_g
            blocks.append(u[rs, sl] * mixed)
        y_g = jnp.concatenate(blocks, axis=0)
        ya_parts.append(_rms(y_g, nao_ref[:, sl]))
    y_ref[:, 0:D_A] = jnp.concatenate(ya_parts, axis=-1).astype(BF16)

    base = 2 * D_A
    zq = z_ref[:, base:base + D_B]
    zf = z_ref[:, base + D_B:base + 2 * D_B]
    v_ref[...] = z_ref[:, base + 2 * D_B:base + 3 * D_B].astype(BF16)

    lbs = lbs_ref[...]
    sm = jnp.exp(lbs - jnp.max(lbs, axis=0, keepdims=True))
    sm = sm / jnp.sum(sm, axis=0, keepdims=True)
    lb = jnp.sum(sm[1:layer + 1, :], axis=0, keepdims=True) if layer > 0 else jnp.zeros((1, D_B), F32)

    soft2 = jnp.log2(1.0 + jnp.exp2(-jnp.abs(zf) * LOG2E))
    log2_sig = jnp.minimum(zf, 0.0) * LOG2E - soft2
    if layer == 0:
        g2 = log2_sig
    else:
        t0 = jnp.log2(lb)
        t1 = jnp.log1p(-lb) * LOG2E + log2_sig
        g2 = jnp.maximum(t0, t1) + jnp.log2(1.0 + jnp.exp2(-jnp.abs(t0 - t1)))
    f = jnp.exp2(g2)
    kk = 1.0 - f

    g_hi = g2.astype(BF16)
    r1 = g2 - g_hi.astype(F32)
    g_mid = r1.astype(BF16)
    g_lo = (r1 - g_mid.astype(F32)).astype(BF16)
    ltri = (lax.broadcasted_iota(jnp.int32, (B_CHUNK, B_CHUNK), 0)
            >= lax.broadcasted_iota(jnp.int32, (B_CHUNK, B_CHUNK), 1)).astype(BF16)
    b_parts = []
    for c in range(rows // B_CHUNK):
        rs = slice(c * B_CHUNK, (c + 1) * B_CHUNK)
        b_parts.append(_dot(ltri, g_hi[rs]) + _dot(ltri, g_mid[rs]) + _dot(ltri, g_lo[rs]))
    b = jnp.concatenate(b_parts, axis=0)

    ks_ref[0] = kk.astype(BF16)
    qs_ref[0] = (zq * f).astype(BF16)
    for lvl in range(1, N_LEVELS):
        e = _level_decay(b, 1 << lvl)
        qs_ref[lvl] = (zq * e).astype(BF16)
        ks_ref[lvl] = (kk * e).astype(BF16)
    qs_ref[N_LEVELS] = zq.astype(BF16)

    b3 = b.reshape(rows // B_CHUNK, B_CHUNK, D_B)
    b_last = b3[:, B_CHUNK - 1:B_CHUNK, :]
    qe_ref[...] = (zq * jnp.exp2(b)).astype(BF16)
    ke_ref[...] = (kk.reshape(b3.shape) * jnp.exp2(b_last - b3)).reshape(rows, D_B).astype(BF16)
    eb_ref[...] = jnp.exp2(b_last).reshape(rows // B_CHUNK, D_B)

    ri = lax.broadcasted_iota(jnp.int32, (B_CHUNK, B_CHUNK), 0)
    ci = lax.broadcasted_iota(jnp.int32, (B_CHUNK, B_CHUNK), 1)
    xor = ri ^ ci
    masks = [(ri > ci) & (xor >= (1 << lvl)) & (xor < (2 << lvl)) for lvl in range(N_LEVELS)]
    masks.append(ri == ci)

    for c in range(rows // B_CHUNK):
        rs = slice(c * B_CHUNK, (c + 1) * B_CHUNK)
        decay = eb_ref[c:c + 1, :]
        for hd in range(B_HEADS):
            sl = slice(hd * HEAD, (hd + 1) * HEAD)
            scores = jnp.zeros((B_CHUNK, B_CHUNK), F32)
            for lvl in range(N_LEVELS + 1):
                k_lvl = ks_ref[0 if lvl == N_LEVELS else lvl, rs, sl]
                scores = jnp.where(masks[lvl], _dot_nt(qs_ref[lvl, rs, sl], k_lvl), scores)
            vh = v_ref[rs, sl]
            st = st_ref[hd]
            o = _dot(scores.astype(BF16), vh) + _dot_nt(qe_ref[rs, sl], st.astype(BF16))
            ob_ref[rs, sl] = o
            st_ref[hd] = st * decay[:, sl] + _dot_tn(vh, ke_ref[rs, sl])

    zg = z_ref[:, base + 3 * D_B:base + 4 * D_B]
    gate = zg / (1.0 + jnp.exp2(zg * -LOG2E))
    y_ref[:, D_A:] = (_group_rms(ob_ref[...], nbo_ref[...], B_HEADS) * gate).astype(BF16)

    o_ref[...] = x + _dot(y_ref[...], wout_ref[...])


def _mlp_kernel(final, x_ref, nw_ref, wup_ref, wdn_ref, nf_ref, o_ref):
    x = x_ref[...]
    h = _rms(x, nw_ref[...]).astype(BF16)
    acc = x
    for j in range(D_FF // FF_BLOCK):
        cs = slice(j * FF_BLOCK, (j + 1) * FF_BLOCK)
        a = jnp.maximum(_dot(h, wup_ref[:, cs]), 0.0)
        acc = acc + _dot((a * a).astype(BF16), wdn_ref[cs, :])
    if final:
        acc = _rms(acc, nf_ref[...])
    o_ref[...] = acc


def _full(shape):
    return pl.BlockSpec(shape, lambda *_: (0,) * len(shape))


def _mixer(layer, x, nmix, win, sw, sbt, nv, nao, lbs, nbo, wout):
    bsz, seq, _ = x.shape
    rows = MIX_ROWS
    n_chunks = rows // B_CHUNK
    xspec = pl.BlockSpec((None, rows, D_MODEL), lambda bi, si: (bi, si, 0))
    return pl.pallas_call(
        functools.partial(_mixer_kernel, layer),
        grid=(bsz, seq // rows),
        in_specs=[xspec, _full(nmix.shape), _full(win.shape), _full(sw.shape), _full(sbt.shape),
                  _full(nv.shape), _full(nao.shape), _full(lbs.shape), _full(nbo.shape),
                  _full(wout.shape)],
        out_specs=xspec,
        out_shape=jax.ShapeDtypeStruct(x.shape, F32),
        scratch_shapes=[
            pltpu.VMEM((B_HEADS, HEAD, HEAD), F32),
            pltpu.VMEM((rows, D_IN), F32),
            pltpu.VMEM((N_LEVELS + 1, rows, D_B), BF16),
            pltpu.VMEM((N_LEVELS, rows, D_B), BF16),
            pltpu.VMEM((rows, D_B), BF16),
            pltpu.VMEM((rows, D_B), BF16),
            pltpu.VMEM((rows, D_B), BF16),
            pltpu.VMEM((n_chunks, D_B), F32),
            pltpu.VMEM((rows, D_B), F32),
            pltpu.VMEM((rows, D_MODEL), BF16),
        ],
        compiler_params=pltpu.CompilerParams(
            dimension_semantics=("arbitrary", "arbitrary"), vmem_limit_bytes=VMEM_LIMIT_BYTES),
        name=f"mixer{layer}",
    )(x, nmix, win, sw, sbt, nv, nao, lbs, nbo, wout)


def _mlp(layer, final, x2, nw, wup, wdn, nf):
    n_tok = x2.shape[0]
    xspec = pl.BlockSpec((MLP_ROWS, D_MODEL), lambda i: (i, 0))
    return pl.pallas_call(
        functools.partial(_mlp_kernel, final),
        grid=(n_tok // MLP_ROWS,),
        in_specs=[xspec, _full(nw.shape), _full(wup.shape), _full(wdn.shape), _full(nf.shape)],
        out_specs=xspec,
        out_shape=jax.ShapeDtypeStruct(x2.shape, F32),
        compiler_params=pltpu.CompilerParams(
            dimension_semantics=("arbitrary",), vmem_limit_bytes=VMEM_LIMIT_BYTES),
        name=f"mlp{layer}",
    )(x2, nw, wup, wdn, nf)


def kernel(x, norm_mix, w_in, spatial_w, spatial_b, norm_v, norm_a_out, lower_bounds, norm_b_out,
           w_out, norm_mlp, w_up, w_down, norm_final):
    bsz, seq, d = x.shape
    depth = w_in.shape[0]
    assert d == D_MODEL and seq % MIX_ROWS == 0 and (bsz * seq) % MLP_ROWS == 0
    row = lambda a: a.reshape(1, -1).astype(F32)
    for l in range(depth):
        x = _mixer(l, x, row(norm_mix[l]), w_in[l].astype(BF16), spatial_w[l], spatial_b[l].T,
                   row(norm_v[l]), row(norm_a_out[l]), lower_bounds.astype(F32), row(norm_b_out[l]),
                   w_out[l].astype(BF16))
        x2 = _mlp(l, l == depth - 1, x.reshape(bsz * seq, d), row(norm_mlp[l]),
                  w_up[l].astype(BF16), w_down[l].astype(BF16), row(norm_final))
        x = x2.reshape(bsz, seq, d)
    return x
```

```python
import functools

import jax
import jax.numpy as jnp
from jax import lax
from jax.experimental import pallas as pl
from jax.experimental.pallas import tpu as pltpu

D_MODEL = 1024
D_A = 512
D_B = 512
A_GROUPS = 4
A_CHUNK = 128
B_HEADS = 4
HEAD = 128
B_CHUNK = 128
D_IN = 2 * D_A + 4 * D_B
D_FF = 4 * D_MODEL
EPS = 1e-6
LOG2E = 1.4426950408889634
SQRT_2_OVER_PI = 0.7978845608028654

SUBLANES = 8
N_LEVELS = 7
VMEM_LIMIT_BYTES = 56 * 1024 * 1024

MIX_ROWS = 256
MLP_ROWS = 512
FF_BLOCK = 1024

BF16 = jnp.bfloat16
F32 = jnp.float32


def _dot(a, b):
    return jnp.dot(a, b, preferred_element_type=F32)


def _dot_tn(a, b):
    return lax.dot_general(a, b, (((0,), (0,)), ((), ())), preferred_element_type=F32)


def _rms(x, w):
    return x * lax.rsqrt(jnp.mean(x * x, axis=-1, keepdims=True) + EPS) * w


def _group_rms(x, w, groups):
    width = x.shape[-1] // groups
    parts = []
    for g in range(groups):
        sl = slice(g * width, (g + 1) * width)
        parts.append(_rms(x[:, sl], w[:, sl]))
    return jnp.concatenate(parts, axis=-1)


def _gelu(x):
    a = -2.0 * SQRT_2_OVER_PI * LOG2E
    return x / (1.0 + jnp.exp2(x * (a + (a * 0.044715) * (x * x))))


def _level_decay(b, half):
    rows, width = b.shape
    n = 2 * half
    if half >= SUBLANES:
        b4 = b.reshape(rows // n, 2, half, width)
        first, second = b4[:, 0], b4[:, 1]
        mid = first[:, half - 1:half, :]
        e = jnp.stack([jnp.exp2(mid - first), jnp.exp2(second - mid)], axis=1)
        return e.reshape(rows, width)
    b3 = b.reshape(rows // SUBLANES, SUBLANES, width)
    if half == SUBLANES // 2:
        mid = jnp.broadcast_to(b3[:, half - 1:half, :], b3.shape)
    else:
        assert half == 2
        pos = lax.broadcasted_iota(jnp.int32, b3.shape, 1) % n
        prev1 = pltpu.roll(b3, 1, 1)
        nxt1 = pltpu.roll(b3, SUBLANES - 1, 1)
        prev2 = pltpu.roll(b3, 2, 1)
        mid = jnp.where(pos == 0, nxt1, jnp.where(pos == 1, b3, jnp.where(pos == 2, prev1, prev2)))
    d = b3 - mid
    return jnp.exp2(jnp.minimum(d, -d)).reshape(rows, width)


def _mixer_kernel(layer, x_ref, nmix_ref, win_ref, sw_ref, sbt_ref, nv_ref, nao_ref, lbs_ref,
                  nbo_ref, wout_ref, o_ref,
                  st_ref, z_ref, qs_ref, kst_ref, qe_ref, ke_ref, v_ref, eb_ref, ob_ref, y_ref):
    rows = x_ref.shape[0]

    @pl.when(pl.program_id(1) == 0)
    def _():
        st_ref[...] = jnp.zeros_like(st_ref)

    x = x_ref[...]
    h = _rms(x, nmix_ref[...])
    z_ref[...] = _dot(h.astype(BF16), win_ref[...])

    u = _gelu(z_ref[:, 0:D_A])
    v = _group_rms(_gelu(z_ref[:, D_A:2 * D_A]), nv_ref[...], A_GROUPS).astype(BF16)
    tri = (lax.broadcasted_iota(jnp.int32, (A_CHUNK, A_CHUNK), 0)
           >= lax.broadcasted_iota(jnp.int32, (A_CHUNK, A_CHUNK), 1))
    ya_parts = []
    for g in range(A_GROUPS):
        w_g = jnp.where(tri, sw_ref[g], 0.0).astype(BF16)
        bias_g = sbt_ref[:, g:g + 1]
        sl = slice(g * HEAD, (g + 1) * HEAD)
        blocks = []
        for c in range(rows // A_CHUNK):
            rs = slice(c * A_CHUNK, (c + 1) * A_CHUNK)
            mixed = _dot(w_g, v[rs, sl]) + bias_g
            blocks.append(u[rs, sl] * mixed)
        y_g = jnp.concatenate(blocks, axis=0)
        ya_parts.append(_rms(y_g, nao_ref[:, sl]))
    y_ref[:, 0:D_A] = jnp.concatenate(ya_parts, axis=-1).astype(BF16)

    base = 2 * D_A
    zq = z_ref[:, base:base + D_B]
    zf = z_ref[:, base + D_B:base + 2 * D_B]
    v_ref[...] = z_ref[:, base + 2 * D_B:base + 3 * D_B].astype(BF16)

    soft2 = jnp.log2(1.0 + jnp.exp2(-jnp.abs(zf) * LOG2E))
    log2_sig = jnp.minimum(zf, 0.0) * LOG2E - soft2
    if layer == 0:
        g2 = log2_sig
    else:
        lbs = lbs_ref[...]
        sm = jnp.exp(lbs - jnp.max(lbs, axis=0, keepdims=True))
        sm = sm / jnp.sum(sm, axis=0, keepdims=True)
        lb = jnp.sum(sm[1:layer + 1, :], axis=0, keepdims=True)
        t0 = jnp.log2(lb)
        t1 = jnp.log1p(-lb) * LOG2E + log2_sig
        g2 = jnp.maximum(t0, t1) + jnp.log2(1.0 + jnp.exp2(-jnp.abs(t0 - t1)))
    f = jnp.exp2(g2)
    kk = 1.0 - f

    g_hi = g2.astype(BF16)
    r1 = g2 - g_hi.astype(F32)
    g_mid = r1.astype(BF16)
    g_lo = (r1 - g_mid.astype(F32)).astype(BF16)
    ltri = (lax.broadcasted_iota(jnp.int32, (B_CHUNK, B_CHUNK), 0)
            >= lax.broadcasted_iota(jnp.int32, (B_CHUNK, B_CHUNK), 1)).astype(BF16)
    b_parts = []
    for c in range(rows // B_CHUNK):
        rs = slice(c * B_CHUNK, (c + 1) * B_CHUNK)
        b_parts.append(_dot(ltri, g_hi[rs]) + _dot(ltri, g_mid[rs]) + _dot(ltri, g_lo[rs]))
    b = jnp.concatenate(b_parts, axis=0)

    def put_kt(lvl, k_lvl):
        for hd in range(B_HEADS):
            kst_ref[lvl, hd] = k_lvl[:, hd * HEAD:(hd + 1) * HEAD].T.astype(BF16)

    put_kt(0, kk)
    qs_ref[0] = (zq * f).astype(BF16)
    for lvl in range(1, N_LEVELS):
        e = _level_decay(b, 1 << lvl)
        qs_ref[lvl] = (zq * e).astype(BF16)
        put_kt(lvl, kk * e)
    qs_ref[N_LEVELS] = zq.astype(BF16)

    b3 = b.reshape(rows // B_CHUNK, B_CHUNK, D_B)
    b_last = b3[:, B_CHUNK - 1:B_CHUNK, :]
    qe_ref[...] = (zq * jnp.exp2(b)).astype(BF16)
    ke_ref[...] = (kk.reshape(b3.shape) * jnp.exp2(b_last - b3)).reshape(rows, D_B).astype(BF16)
    eb_ref[...] = jnp.exp2(b_last).reshape(rows // B_CHUNK, D_B)

    ri = lax.broadcasted_iota(jnp.int32, (B_CHUNK, B_CHUNK), 0)
    ci = lax.broadcasted_iota(jnp.int32, (B_CHUNK, B_CHUNK), 1)
    xor = ri ^ ci
    masks = [(ri > ci) & (xor >= (1 << lvl)) & (xor < (2 << lvl)) for lvl in range(N_LEVELS)]
    masks.append(ri == ci)

    for c in range(rows // B_CHUNK):
        rs = slice(c * B_CHUNK, (c + 1) * B_CHUNK)
        decay = eb_ref[c:c + 1, :]
        for hd in range(B_HEADS):
            sl = slice(hd * HEAD, (hd + 1) * HEAD)
            scores = jnp.zeros((B_CHUNK, B_CHUNK), F32)
            for lvl in range(N_LEVELS + 1):
                k_t = kst_ref[0 if lvl == N_LEVELS else lvl, hd, :, rs]
                scores = jnp.where(masks[lvl], _dot(qs_ref[lvl, rs, sl], k_t), scores)
            vh = v_ref[rs, sl]
            st = st_ref[hd]
            o = _dot(scores.astype(BF16), vh) + _dot(qe_ref[rs, sl], st.T.astype(BF16))
            ob_ref[rs, sl] = o
            st_ref[hd] = st * decay[:, sl] + _dot_tn(vh, ke_ref[rs, sl])

    zg = z_ref[:, base + 3 * D_B:base + 4 * D_B]
    gate = zg / (1.0 + jnp.exp2(zg * -LOG2E))
    y_ref[:, D_A:] = (_group_rms(ob_ref[...], nbo_ref[...], B_HEADS) * gate).astype(BF16)

    o_ref[...] = x + _dot(y_ref[...], wout_ref[...])


def _mlp_kernel(final, x_ref, nw_ref, wup_ref, wdn_ref, nf_ref, o_ref):
    x = x_ref[...]
    h = _rms(x, nw_ref[...]).astype(BF16)
    acc = x
    for j in range(D_FF // FF_BLOCK):
        cs = slice(j * FF_BLOCK, (j + 1) * FF_BLOCK)
        a = jnp.maximum(_dot(h, wup_ref[:, cs]), 0.0)
        acc = acc + _dot((a * a).astype(BF16), wdn_ref[cs, :])
    if final:
        acc = _rms(acc, nf_ref[...])
    o_ref[...] = acc


def _full(shape):
    return pl.BlockSpec(shape, lambda *_: (0,) * len(shape))


def _mixer(layer, x, nmix, win, sw, sbt, nv, nao, lbs, nbo, wout):
    bsz, seq, _ = x.shape
    rows = MIX_ROWS
    n_chunks = rows // B_CHUNK
    xspec = pl.BlockSpec((None, rows, D_MODEL), lambda bi, si: (bi, si, 0))
    return pl.pallas_call(
        functools.partial(_mixer_kernel, layer),
        grid=(bsz, seq // rows),
        in_specs=[xspec, _full(nmix.shape), _full(win.shape), _full(sw.shape), _full(sbt.shape),
                  _full(nv.shape), _full(nao.shape), _full(lbs.shape), _full(nbo.shape),
                  _full(wout.shape)],
        out_specs=xspec,
        out_shape=jax.ShapeDtypeStruct(x.shape, F32),
        scratch_shapes=[
            pltpu.VMEM((B_HEADS, HEAD, HEAD), F32),
            pltpu.VMEM((rows, D_IN), F32),
            pltpu.VMEM((N_LEVELS + 1, rows, D_B), BF16),
            pltpu.VMEM((N_LEVELS, B_HEADS, HEAD, rows), BF16),
            pltpu.VMEM((rows, D_B), BF16),
            pltpu.VMEM((rows, D_B), BF16),
            pltpu.VMEM((rows, D_B), BF16),
            pltpu.VMEM((n_chunks, D_B), F32),
            pltpu.VMEM((rows, D_B), F32),
            pltpu.VMEM((rows, D_MODEL), BF16),
        ],
        compiler_params=pltpu.CompilerParams(
            dimension_semantics=("arbitrary", "arbitrary"), vmem_limit_bytes=VMEM_LIMIT_BYTES),
        name=f"mixer{layer}",
    )(x, nmix, win, sw, sbt, nv, nao, lbs, nbo, wout)


def _mlp(layer, final, x2, nw, wup, wdn, nf):
    n_tok = x2.shape[0]
    xspec = pl.BlockSpec((MLP_ROWS, D_MODEL), lambda i: (i, 0))
    return pl.pallas_call(
        functools.partial(_mlp_kernel, final),
        grid=(n_tok // MLP_ROWS,),
        in_specs=[xspec, _full(nw.shape), _full(wup.shape), _full(wdn.shape), _full(nf.shape)],
        out_specs=xspec,
        out_shape=jax.ShapeDtypeStruct(x2.shape, F32),
        compiler_params=pltpu.CompilerParams(
            dimension_semantics=("arbitrary",), vmem_limit_bytes=VMEM_LIMIT_BYTES),
        name=f"mlp{layer}",
    )(x2, nw, wup, wdn, nf)


def kernel(x, norm_mix, w_in, spatial_w, spatial_b, norm_v, norm_a_out, lower_bounds, norm_b_out,
           w_out, norm_mlp, w_up, w_down, norm_final):
    bsz, seq, d = x.shape
    depth = w_in.shape[0]
    assert d == D_MODEL and seq % MIX_ROWS == 0 and (bsz * seq) % MLP_ROWS == 0
    row = lambda a: a.reshape(1, -1).astype(F32)
    for l in range(depth):
        x = _mixer(l, x, row(norm_mix[l]), w_in[l].astype(BF16), spatial_w[l], spatial_b[l].T,
                   row(norm_v[l]), row(norm_a_out[l]), lower_bounds.astype(F32), row(norm_b_out[l]),
                   w_out[l].astype(BF16))
        x2 = _mlp(l, l == depth - 1, x.reshape(bsz * seq, d), row(norm_mlp[l]),
                  w_up[l].astype(BF16), w_down[l].astype(BF16), row(norm_final))
        x = x2.reshape(bsz, seq, d)
    return x
```

```python
import functools

import jax
import jax.numpy as jnp
from jax import lax
from jax.experimental import pallas as pl
from jax.experimental.pallas import tpu as pltpu

D_MODEL = 1024
D_A = 512
D_B = 512
A_GROUPS = 4
A_CHUNK = 128
B_HEADS = 4
HEAD = 128
B_CHUNK = 128
D_IN = 2 * D_A + 4 * D_B
D_FF = 4 * D_MODEL
EPS = 1e-6
LOG2E = 1.4426950408889634
SQRT_2_OVER_PI = 0.7978845608028654

SUBLANES = 8
N_LEVELS = 7
VMEM_LIMIT_BYTES = 56 * 1024 * 1024

MIX_ROWS = 512
MLP_ROWS = 512
FF_BLOCK = 1024

BF16 = jnp.bfloat16
F32 = jnp.float32


def _dot(a, b):
    return jnp.dot(a, b, preferred_element_type=F32)


def _dot_tn(a, b):
    return lax.dot_general(a, b, (((0,), (0,)), ((), ())), preferred_element_type=F32)


def _rms(x, w):
    return x * lax.rsqrt(jnp.mean(x * x, axis=-1, keepdims=True) + EPS) * w


def _group_rms(x, w, groups):
    width = x.shape[-1] // groups
    parts = []
    for g in range(groups):
        sl = slice(g * width, (g + 1) * width)
        parts.append(_rms(x[:, sl], w[:, sl]))
    return jnp.concatenate(parts, axis=-1)


def _gelu(x):
    a = -2.0 * SQRT_2_OVER_PI * LOG2E
    return x / (1.0 + jnp.exp2(x * (a + (a * 0.044715) * (x * x))))


def _level_operands(q, k, b, half):
    rows, width = b.shape
    n = 2 * half
    if half >= SUBLANES:
        shape = (rows // n, 2, half, width)
        b4, q4, k4 = b.reshape(shape), q.reshape(shape), k.reshape(shape)
        first, second = b4[:, 0], b4[:, 1]
        mid = first[:, half - 1:half, :]
        zero = jnp.zeros_like(first)
        qe = jnp.stack([zero, q4[:, 1] * jnp.exp2(second - mid)], axis=1)
        ke = jnp.stack([k4[:, 0] * jnp.exp2(mid - first), zero], axis=1)
        return qe.reshape(rows, width), ke.reshape(rows, width)
    b3 = b.reshape(rows // SUBLANES, SUBLANES, width)
    pos = lax.broadcasted_iota(jnp.int32, b3.shape, 1) % n
    if half == SUBLANES // 2:
        mid = jnp.broadcast_to(b3[:, half - 1:half, :], b3.shape)
    else:
        assert half == 2
        prev1 = pltpu.roll(b3, 1, 1)
        nxt1 = pltpu.roll(b3, SUBLANES - 1, 1)
        prev2 = pltpu.roll(b3, 2, 1)
        mid = jnp.where(pos == 0, nxt1, jnp.where(pos == 1, b3, jnp.where(pos == 2, prev1, prev2)))
    sign = jnp.where(pos < half, -1.0, 1.0)
    e = jnp.exp2((b3 - mid) * sign).reshape(rows, width)
    return q * e, k * e


def _mixer_kernel(layer, x_ref, nmix_ref, win_ref, sw_ref, sbt_ref, nv_ref, nao_ref, lbs_ref,
                  nbo_ref, wout_ref, o_ref,
                  st_ref, z_ref, vn_ref, qs_ref, kst_ref, qe_ref, ke_ref, v_ref, eb_ref, ob_ref, y_ref):
    rows = x_ref.shape[0]

    @pl.when(pl.program_id(1) == 0)
    def _():
        st_ref[...] = jnp.zeros_like(st_ref)

    x = x_ref[...]
    h = _rms(x, nmix_ref[...])
    z_ref[...] = _dot(h.astype(BF16), win_ref[...])

    u = _gelu(z_ref[:, 0:D_A])
    vn_ref[...] = _group_rms(_gelu(z_ref[:, D_A:2 * D_A]), nv_ref[...], A_GROUPS).astype(BF16)
    tri = (lax.broadcasted_iota(jnp.int32, (A_CHUNK, A_CHUNK), 0)
           >= lax.broadcasted_iota(jnp.int32, (A_CHUNK, A_CHUNK), 1))
    ya_parts = []
    for g in range(A_GROUPS):
        w_g = jnp.where(tri, sw_ref[g], 0.0).astype(BF16)
        bias_g = sbt_ref[:, g:g + 1]
        sl = slice(g * HEAD, (g + 1) * HEAD)
        blocks = []
        for c in range(rows // A_CHUNK):
            rs = slice(c * A_CHUNK, (c + 1) * A_CHUNK)
            mixed = _dot(w_g, vn_ref[rs, sl]) + bias_g
            blocks.append(u[rs, sl] * mixed)
        y_g = jnp.concatenate(blocks, axis=0)
        ya_parts.append(_rms(y_g, nao_ref[:, sl]))
    y_ref[:, 0:D_A] = jnp.concatenate(ya_parts, axis=-1).astype(BF16)

    base = 2 * D_A
    zq = z_ref[:, base:base + D_B]
    zf = z_ref[:, base + D_B:base + 2 * D_B]
    v_ref[...] = z_ref[:, base + 2 * D_B:base + 3 * D_B].astype(BF16)

    soft2 = jnp.log2(1.0 + jnp.exp2(-jnp.abs(zf) * LOG2E))
    log2_sig = jnp.minimum(zf, 0.0) * LOG2E - soft2
    if layer == 0:
        g2 = log2_sig
    else:
        lbs = lbs_ref[...]
        sm = jnp.exp(lbs - jnp.max(lbs, axis=0, keepdims=True))
        sm = sm / jnp.sum(sm, axis=0, keepdims=True)
        lb = jnp.sum(sm[1:layer + 1, :], axis=0, keepdims=True)
        t0 = jnp.log2(lb)
        t1 = jnp.log1p(-lb) * LOG2E + log2_sig
        g2 = jnp.maximum(t0, t1) + jnp.log2(1.0 + jnp.exp2(-jnp.abs(t0 - t1)))
    f = jnp.exp2(g2)
    kk = 1.0 - f

    g_hi = g2.astype(BF16)
    g_lo = (g2 - g_hi.astype(F32)).astype(BF16)
    ltri = (lax.broadcasted_iota(jnp.int32, (B_CHUNK, B_CHUNK), 0)
            >= lax.broadcasted_iota(jnp.int32, (B_CHUNK, B_CHUNK), 1)).astype(BF16)
    b_parts = []
    for c in range(rows // B_CHUNK):
        rs = slice(c * B_CHUNK, (c + 1) * B_CHUNK)
        b_parts.append(_dot(ltri, g_hi[rs]) + _dot(ltri, g_lo[rs]))
    b = jnp.concatenate(b_parts, axis=0)

    def put_kt(lvl, k_lvl):
        for hd in range(B_HEADS):
            kst_ref[lvl, hd] = k_lvl[:, hd * HEAD:(hd + 1) * HEAD].T.astype(BF16)

    put_kt(0, kk)
    qs_ref[0] = (zq * f).astype(BF16)
    for lvl in range(1, N_LEVELS):
        q_lvl, k_lvl = _level_operands(zq, kk, b, 1 << lvl)
        qs_ref[lvl] = q_lvl.astype(BF16)
        put_kt(lvl, k_lvl)
    qs_ref[N_LEVELS] = zq.astype(BF16)

    b3 = b.reshape(rows // B_CHUNK, B_CHUNK, D_B)
    b_last = b3[:, B_CHUNK - 1:B_CHUNK, :]
    qe_ref[...] = (zq * jnp.exp2(b)).astype(BF16)
    ke_ref[...] = (kk.reshape(b3.shape) * jnp.exp2(b_last - b3)).reshape(rows, D_B).astype(BF16)
    eb_ref[...] = jnp.exp2(b_last).reshape(rows // B_CHUNK, D_B)

    ri = lax.broadcasted_iota(jnp.int32, (B_CHUNK, B_CHUNK), 0)
    ci = lax.broadcasted_iota(jnp.int32, (B_CHUNK, B_CHUNK), 1)
    xor = ri ^ ci
    masks = [(ri > ci) & (xor >= (1 << lvl)) & (xor < (2 << lvl)) for lvl in range(N_LEVELS)]
    masks.append(ri == ci)

    for c in range(rows // B_CHUNK):
        rs = slice(c * B_CHUNK, (c + 1) * B_CHUNK)
        decay = eb_ref[c:c + 1, :]
        for hd in range(B_HEADS):
            sl = slice(hd * HEAD, (hd + 1) * HEAD)
            scores = jnp.zeros((B_CHUNK, B_CHUNK), F32)
            for lvl in range(N_LEVELS + 1):
                k_t = kst_ref[0 if lvl == N_LEVELS else lvl, hd, :, rs]
                scores = jnp.where(masks[lvl], _dot(qs_ref[lvl, rs, sl], k_t), scores)
            vh = v_ref[rs, sl]
            st = st_ref[hd]
            o = _dot(scores.astype(BF16), vh) + _dot(qe_ref[rs, sl], st.T.astype(BF16))
            ob_ref[rs, sl] = o
            st_ref[hd] = st * decay[:, sl] + _dot_tn(vh, ke_ref[rs, sl])

    zg = z_ref[:, base + 3 * D_B:base + 4 * D_B]
    gate = zg / (1.0 + jnp.exp2(zg * -LOG2E))
    y_ref[:, D_A:] = (_group_rms(ob_ref[...], nbo_ref[...], B_HEADS) * gate).astype(BF16)

    o_ref[...] = x + _dot(y_ref[...], wout_ref[...])


def _mlp_kernel(final, x_ref, nw_ref, wup_ref, wdn_ref, nf_ref, o_ref):
    x = x_ref[...]
    h = _rms(x, nw_ref[...]).astype(BF16)
    acc = x
    for j in range(D_FF // FF_BLOCK):
        cs = slice(j * FF_BLOCK, (j + 1) * FF_BLOCK)
        a = jnp.maximum(_dot(h, wup_ref[:, cs]), 0.0)
        acc = acc + _dot((a * a).astype(BF16), wdn_ref[cs, :])
    if final:
        acc = _rms(acc, nf_ref[...])
    o_ref[...] = acc


def _full(shape):
    return pl.BlockSpec(shape, lambda *_: (0,) * len(shape))


def _mixer(layer, x, nmix, win, sw, sbt, nv, nao, lbs, nbo, wout):
    bsz, seq, _ = x.shape
    rows = MIX_ROWS
    n_chunks = rows // B_CHUNK
    xspec = pl.BlockSpec((None, rows, D_MODEL), lambda bi, si: (bi, si, 0))
    return pl.pallas_call(
        functools.partial(_mixer_kernel, layer),
        grid=(bsz, seq // rows),
        in_specs=[xspec, _full(nmix.shape), _full(win.shape), _full(sw.shape), _full(sbt.shape),
                  _full(nv.shape), _full(nao.shape), _full(lbs.shape), _full(nbo.shape),
                  _full(wout.shape)],
        out_specs=xspec,
        out_shape=jax.ShapeDtypeStruct(x.shape, F32),
        scratch_shapes=[
            pltpu.VMEM((B_HEADS, HEAD, HEAD), F32),
            pltpu.VMEM((rows, D_IN), F32),
            pltpu.VMEM((rows, D_A), BF16),
            pltpu.VMEM((N_LEVELS + 1, rows, D_B), BF16),
            pltpu.VMEM((N_LEVELS, B_HEADS, HEAD, rows), BF16),
            pltpu.VMEM((rows, D_B), BF16),
            pltpu.VMEM((rows, D_B), BF16),
            pltpu.VMEM((rows, D_B), BF16),
            pltpu.VMEM((n_chunks, D_B), F32),
            pltpu.VMEM((rows, D_B), F32),
            pltpu.VMEM((rows, D_MODEL), BF16),
        ],
        compiler_params=pltpu.CompilerParams(
            dimension_semantics=("arbitrary", "arbitrary"), vmem_limit_bytes=VMEM_LIMIT_BYTES),
        name=f"mixer{layer}",
    )(x, nmix, win, sw, sbt, nv, nao, lbs, nbo, wout)


def _mlp(layer, final, x2, nw, wup, wdn, nf):
    n_tok = x2.shape[0]
    xspec = pl.BlockSpec((MLP_ROWS, D_MODEL), lambda i: (i, 0))
    return pl.pallas_call(
        functools.partial(_mlp_kernel, final),
        grid=(n_tok // MLP_ROWS,),
        in_specs=[xspec, _full(nw.shape), _full(wup.shape), _full(wdn.shape), _full(nf.shape)],
        out_specs=xspec,
        out_shape=jax.ShapeDtypeStruct(x2.shape, F32),
        compiler_params=pltpu.CompilerParams(
            dimension_semantics=("arbitrary",), vmem_limit_bytes=VMEM_LIMIT_BYTES),
        name=f"mlp{layer}",
    )(x2, nw, wup, wdn, nf)


def kernel(x, norm_mix, w_in, spatial_w, spatial_b, norm_v, norm_a_out, lower_bounds, norm_b_out,
           w_out, norm_mlp, w_up, w_down, norm_final):
    bsz, seq, d = x.shape
    depth = w_in.shape[0]
    assert d == D_MODEL and seq % MIX_ROWS == 0 and (bsz * seq) % MLP_ROWS == 0
    row = lambda a: a.reshape(1, -1).astype(F32)
    for l in range(depth):
        x = _mixer(l, x, row(norm_mix[l]), w_in[l].astype(BF16), spatial_w[l], spatial_b[l].T,
                   row(norm_v[l]), row(norm_a_out[l]), lower_bounds.astype(F32), row(norm_b_out[l]),
                   w_out[l].astype(BF16))
        x2 = _mlp(l, l == depth - 1, x.reshape(bsz * seq, d), row(norm_mlp[l]),
                  w_up[l].astype(BF16), w_down[l].astype(BF16), row(norm_final))
        x = x2.reshape(bsz, seq, d)
    return x
```

```python
import functools

import jax
import jax.numpy as jnp
from jax import lax
from jax.experimental import pallas as pl
from jax.experimental.pallas import tpu as pltpu

D_MODEL = 1024
D_A = 512
D_B = 512
A_GROUPS = 4
A_CHUNK = 128
B_HEADS = 4
HEAD = 128
B_CHUNK = 128
D_IN = 2 * D_A + 4 * D_B
D_FF = 4 * D_MODEL
EPS = 1e-6
LOG2E = 1.4426950408889634
SQRT_2_OVER_PI = 0.7978845608028654

SUBLANES = 8
N_LEVELS = 7
VMEM_LIMIT_BYTES = 60 * 1024 * 1024

PROJ_ROWS = 512
MIX_ROWS = 512
FF_BLOCK = 1024
MLP_PIECE = 256
MIX_DELAY = 7

BF16 = jnp.bfloat16
F32 = jnp.float32


def _dot(a, b):
    return jnp.dot(a, b, preferred_element_type=F32)


def _dot_tn(a, b):
    return lax.dot_general(a, b, (((0,), (0,)), ((), ())), preferred_element_type=F32)


def _rms(x, w):
    return x * lax.rsqrt(jnp.mean(x * x, axis=-1, keepdims=True) + EPS) * w


def _group_rms(x, w, groups):
    width = x.shape[-1] // groups
    parts = []
    for g in range(groups):
        sl = slice(g * width, (g + 1) * width)
        parts.append(_rms(x[:, sl], w[:, sl]))
    return jnp.concatenate(parts, axis=-1)


def _gelu(x):
    a = -2.0 * SQRT_2_OVER_PI * LOG2E
    return x / (1.0 + jnp.exp2(x * (a + (a * 0.044715) * (x * x))))


def _level_operands(q, k, b, half):
    rows, width = b.shape
    n = 2 * half
    if half >= SUBLANES:
        shape = (rows // n, 2, half, width)
        b4, q4, k4 = b.reshape(shape), q.reshape(shape), k.reshape(shape)
        first, second = b4[:, 0], b4[:, 1]
        mid = first[:, half - 1:half, :]
        zero = jnp.zeros_like(first)
        qe = jnp.stack([zero, q4[:, 1] * jnp.exp2(second - mid)], axis=1)
        ke = jnp.stack([k4[:, 0] * jnp.exp2(mid - first), zero], axis=1)
        return qe.reshape(rows, width), ke.reshape(rows, width)
    b3 = b.reshape(rows // SUBLANES, SUBLANES, width)
    pos = lax.broadcasted_iota(jnp.int32, b3.shape, 1) % n
    if half == SUBLANES // 2:
        mid = jnp.broadcast_to(b3[:, half - 1:half, :], b3.shape)
    else:
        assert half == 2
        prev1 = pltpu.roll(b3, 1, 1)
        nxt1 = pltpu.roll(b3, SUBLANES - 1, 1)
        prev2 = pltpu.roll(b3, 2, 1)
        mid = jnp.where(pos == 0, nxt1, jnp.where(pos == 1, b3, jnp.where(pos == 2, prev1, prev2)))
    sign = jnp.where(pos < half, -1.0, 1.0)
    e = jnp.exp2((b3 - mid) * sign).reshape(rows, width)
    return q * e, k * e


def _proj_kernel(x_ref, nmix_ref, win_ref, z_ref):
    z_ref[...] = _dot(_rms(x_ref[...], nmix_ref[...]).astype(BF16), win_ref[...])


def _mix_stages(layer, x_ref, z_ref, sw_ref, sbt_ref, nv_ref, nao_ref, lbs_ref, nbo_ref, wout_ref,
                xmid_ref, st_ref, vn_ref, qs_ref, kst_ref, qe_ref, ke_ref, v_ref, eb_ref, ob_ref,
                y_ref):
    rows = x_ref.shape[0]

    u = _gelu(z_ref[:, 0:D_A])
    yield
    gv = _gelu(z_ref[:, D_A:2 * D_A])
    yield
    vn_ref[...] = _group_rms(gv, nv_ref[...], A_GROUPS).astype(BF16)
    yield
    tri = (lax.broadcasted_iota(jnp.int32, (A_CHUNK, A_CHUNK), 0)
           >= lax.broadcasted_iota(jnp.int32, (A_CHUNK, A_CHUNK), 1))
    for g in range(A_GROUPS):
        w_g = jnp.where(tri, sw_ref[g], 0.0).astype(BF16)
        bias_g = sbt_ref[:, g:g + 1]
        sl = slice(g * HEAD, (g + 1) * HEAD)
        blocks = []
        for c in range(rows // A_CHUNK):
            rs = slice(c * A_CHUNK, (c + 1) * A_CHUNK)
            mixed = _dot(w_g, vn_ref[rs, sl]) + bias_g
            blocks.append(u[rs, sl] * mixed)
        y_g = jnp.concatenate(blocks, axis=0)
        y_ref[:, sl] = _rms(y_g, nao_ref[:, sl]).astype(BF16)
        if g % 2 == 1:
            yield

    base = 2 * D_A
    zf = z_ref[:, base + D_B:base + 2 * D_B]
    v_ref[...] = z_ref[:, base + 2 * D_B:base + 3 * D_B].astype(BF16)

    soft2 = jnp.log2(1.0 + jnp.exp2(-jnp.abs(zf) * LOG2E))
    log2_sig = jnp.minimum(zf, 0.0) * LOG2E - soft2
    yield
    if layer == 0:
        g2 = log2_sig
    else:
        lbs = lbs_ref[...]
        sm = jnp.exp(lbs - jnp.max(lbs, axis=0, keepdims=True))
        sm = sm / jnp.sum(sm, axis=0, keepdims=True)
        lb = jnp.sum(sm[1:layer + 1, :], axis=0, keepdims=True)
        t0 = jnp.log2(lb)
        t1 = jnp.log1p(-lb) * LOG2E + log2_sig
        g2 = jnp.maximum(t0, t1) + jnp.log2(1.0 + jnp.exp2(-jnp.abs(t0 - t1)))
        yield
    f = jnp.exp2(g2)
    kk = 1.0 - f

    g_hi = g2.astype(BF16)
    g_lo = (g2 - g_hi.astype(F32)).astype(BF16)
    ltri = (lax.broadcasted_iota(jnp.int32, (B_CHUNK, B_CHUNK), 0)
            >= lax.broadcasted_iota(jnp.int32, (B_CHUNK, B_CHUNK), 1)).astype(BF16)
    b_parts = []
    for c in range(rows // B_CHUNK):
        rs = slice(c * B_CHUNK, (c + 1) * B_CHUNK)
        b_parts.append(_dot(ltri, g_hi[rs]) + _dot(ltri, g_lo[rs]))
    b = jnp.concatenate(b_parts, axis=0)
    yield

    def put_kt(lvl, k_lvl):
        for hd in range(B_HEADS):
            kst_ref[lvl, hd] = k_lvl[:, hd * HEAD:(hd + 1) * HEAD].T.astype(BF16)

    zq = z_ref[:, base:base + D_B]
    put_kt(0, kk)
    qs_ref[0] = (zq * f).astype(BF16)
    qs_ref[N_LEVELS] = zq.astype(BF16)
    yield
    for lvl in range(1, N_LEVELS):
        q_lvl, k_lvl = _level_operands(zq, kk, b, 1 << lvl)
        qs_ref[lvl] = q_lvl.astype(BF16)
        put_kt(lvl, k_lvl)
        yield

    b3 = b.reshape(rows // B_CHUNK, B_CHUNK, D_B)
    b_last = b3[:, B_CHUNK - 1:B_CHUNK, :]
    qe_ref[...] = (zq * jnp.exp2(b)).astype(BF16)
    ke_ref[...] = (kk.reshape(b3.shape) * jnp.exp2(b_last - b3)).reshape(rows, D_B).astype(BF16)
    eb_ref[...] = jnp.exp2(b_last).reshape(rows // B_CHUNK, D_B)
    yield

    zg = z_ref[:, base + 3 * D_B:base + 4 * D_B]
    gate = zg / (1.0 + jnp.exp2(zg * -LOG2E))
    yield

    ri = lax.broadcasted_iota(jnp.int32, (B_CHUNK, B_CHUNK), 0)
    ci = lax.broadcasted_iota(jnp.int32, (B_CHUNK, B_CHUNK), 1)
    xor = ri ^ ci
    masks = [(ri > ci) & (xor >= (1 << lvl)) & (xor < (2 << lvl)) for lvl in range(N_LEVELS)]
    masks.append(ri == ci)

    for c in range(rows // B_CHUNK):
        rs = slice(c * B_CHUNK, (c + 1) * B_CHUNK)
        decay = eb_ref[c:c + 1, :]
        for hd in range(B_HEADS):
            sl = slice(hd * HEAD, (hd + 1) * HEAD)
            scores = jnp.zeros((B_CHUNK, B_CHUNK), F32)
            for lvl in range(N_LEVELS + 1):
                k_t = kst_ref[0 if lvl == N_LEVELS else lvl, hd, :, rs]
                scores = jnp.where(masks[lvl], _dot(qs_ref[lvl, rs, sl], k_t), scores)
            vh = v_ref[rs, sl]
            st = st_ref[hd]
            o = _dot(scores.astype(BF16), vh) + _dot(qe_ref[rs, sl], st.T.astype(BF16))
            ob_ref[rs, sl] = o
            st_ref[hd] = st * decay[:, sl] + _dot_tn(vh, ke_ref[rs, sl])
            if hd % 2 == 1:
                yield

    y_ref[:, D_A:] = (_group_rms(ob_ref[...], nbo_ref[...], B_HEADS) * gate).astype(BF16)
    yield
    xmid_ref[...] = x_ref[...] + _dot(y_ref[...], wout_ref[...])


def _mlp_stages(final, xmid_ref, nw_ref, wup_ref, wdn_ref, nf_ref, o_ref, a_ref):
    x = xmid_ref[...]
    h = _rms(x, nw_ref[...]).astype(BF16)
    o_ref[...] = x
    yield
    pieces = FF_BLOCK // MLP_PIECE
    for j in range(D_FF // FF_BLOCK):
        for p in range(pieces):
            cs = slice(j * FF_BLOCK + p * MLP_PIECE, j * FF_BLOCK + (p + 1) * MLP_PIECE)
            a = jnp.maximum(_dot(h, wup_ref[:, cs]), 0.0)
            a_ref[:, p * MLP_PIECE:(p + 1) * MLP_PIECE] = (a * a).astype(BF16)
            yield
        for p in range(D_MODEL // MLP_PIECE):
            cs = slice(p * MLP_PIECE, (p + 1) * MLP_PIECE)
            o_ref[:, cs] += _dot(a_ref[...], wdn_ref[j * FF_BLOCK:(j + 1) * FF_BLOCK, cs])
            yield
    if final:
        o_ref[...] = _rms(o_ref[...], nf_ref[...])


def _interleave(main, side, delay):
    for i, _ in enumerate(main):
        if i >= delay:
            next(side, None)
    for _ in side:
        pass


def _layer_kernel(layer, final, tiles_per_seq, x_ref, z_ref, sw_ref, sbt_ref, nv_ref, nao_ref,
                  lbs_ref, nbo_ref, wout_ref, nw_ref, wup_ref, wdn_ref, nf_ref, o_ref,
                  xmid_ref, a_ref, st_ref, *work):
    step = pl.program_id(0)

    @pl.when(step == 0)
    def _():
        xmid_ref[...] = jnp.zeros_like(xmid_ref)

    @pl.when(step % tiles_per_seq == 0)
    def _():
        st_ref[...] = jnp.zeros_like(st_ref)

    _interleave(_mlp_stages(final, xmid_ref, nw_ref, wup_ref, wdn_ref, nf_ref, o_ref, a_ref),
                _mix_stages(layer, x_ref, z_ref, sw_ref, sbt_ref, nv_ref, nao_ref, lbs_ref, nbo_ref,
                            wout_ref, xmid_ref, st_ref, *work), MIX_DELAY)


def _full(shape):
    return pl.BlockSpec(shape, lambda *_: (0,) * len(shape))


def _proj(layer, x2, nmix, win):
    n_tok = x2.shape[0]
    return pl.pallas_call(
        _proj_kernel,
        grid=(n_tok // PROJ_ROWS,),
        in_specs=[pl.BlockSpec((PROJ_ROWS, D_MODEL), lambda i: (i, 0)), _full(nmix.shape),
                  _full(win.shape)],
        out_specs=pl.BlockSpec((PROJ_ROWS, D_IN), lambda i: (i, 0)),
        out_shape=jax.ShapeDtypeStruct((n_tok, D_IN), F32),
        compiler_params=pltpu.CompilerParams(
            dimension_semantics=("arbitrary",), vmem_limit_bytes=VMEM_LIMIT_BYTES),
        name=f"proj{layer}",
    )(x2, nmix, win)


def _resident(shape):
    return pl.BlockSpec(shape, lambda *_: (0,) * len(shape), pipeline_mode=pl.Buffered(1))


def _layer(layer, final, x2, z, seq, sw, sbt, nv, nao, lbs, nbo, wout, nw, wup, wdn, nf):
    n_tok = x2.shape[0]
    rows = MIX_ROWS
    n_tiles = n_tok // rows
    n_chunks = rows // B_CHUNK
    cur = lambda i: (jnp.minimum(i, n_tiles - 1), 0)
    prev = lambda i: (jnp.maximum(i - 1, 0), 0)
    return pl.pallas_call(
        functools.partial(_layer_kernel, layer, final, seq // rows),
        grid=(n_tiles + 1,),
        in_specs=[pl.BlockSpec((rows, D_MODEL), cur), pl.BlockSpec((rows, D_IN), cur),
                  _full(sw.shape), _full(sbt.shape), _full(nv.shape), _full(nao.shape),
                  _full(lbs.shape), _full(nbo.shape), _resident(wout.shape), _full(nw.shape),
                  _resident(wup.shape), _resident(wdn.shape), _full(nf.shape)],
        out_specs=pl.BlockSpec((rows, D_MODEL), prev),
        out_shape=jax.ShapeDtypeStruct(x2.shape, F32),
        scratch_shapes=[
            pltpu.VMEM((rows, D_MODEL), F32),
            pltpu.VMEM((rows, FF_BLOCK), BF16),
            pltpu.VMEM((B_HEADS, HEAD, HEAD), F32),
            pltpu.VMEM((rows, D_A), BF16),
            pltpu.VMEM((N_LEVELS + 1, rows, D_B), BF16),
            pltpu.VMEM((N_LEVELS, B_HEADS, HEAD, rows), BF16),
            pltpu.VMEM((rows, D_B), BF16),
            pltpu.VMEM((rows, D_B), BF16),
            pltpu.VMEM((rows, D_B), BF16),
            pltpu.VMEM((n_chunks, D_B), F32),
            pltpu.VMEM((rows, D_B), F32),
            pltpu.VMEM((rows, D_MODEL), BF16),
        ],
        compiler_params=pltpu.CompilerParams(
            dimension_semantics=("arbitrary",), vmem_limit_bytes=VMEM_LIMIT_BYTES),
        name=f"layer{layer}",
    )(x2, z, sw, sbt, nv, nao, lbs, nbo, wout, nw, wup, wdn, nf)


def kernel(x, norm_mix, w_in, spatial_w, spatial_b, norm_v, norm_a_out, lower_bounds, norm_b_out,
           w_out, norm_mlp, w_up, w_down, norm_final):
    bsz, seq, d = x.shape
    depth = w_in.shape[0]
    n_tok = bsz * seq
    assert d == D_MODEL and seq % MIX_ROWS == 0 and n_tok % PROJ_ROWS == 0
    row = lambda a: a.reshape(1, -1).astype(F32)
    x2 = x.reshape(n_tok, d)
    for l in range(depth):
        z = _proj(l, x2, row(norm_mix[l]), w_in[l].astype(BF16))
        x2 = _layer(l, l == depth - 1, x2, z, seq, spatial_w[l], spatial_b[l].T, row(norm_v[l]),
                    row(norm_a_out[l]), lower_bounds.astype(F32), row(norm_b_out[l]),
                    w_out[l].astype(BF16), row(norm_mlp[l]), w_up[l].astype(BF16),
                    w_down[l].astype(BF16), row(norm_final))
    return x2.reshape(bsz, seq, d)
```

```python
import functools

import jax
import jax.numpy as jnp
from jax import lax
from jax.experimental import pallas as pl
from jax.experimental.pallas import tpu as pltpu

D_MODEL = 1024
D_A = 512
D_B = 512
A_GROUPS = 4
A_CHUNK = 128
B_HEADS = 4
HEAD = 128
B_CHUNK = 128
D_IN = 2 * D_A + 4 * D_B
D_FF = 4 * D_MODEL
EPS = 1e-6
LOG2E = 1.4426950408889634
SQRT_2_OVER_PI = 0.7978845608028654

SUBLANES = 8
N_LEVELS = 7
VMEM_LIMIT_BYTES = 60 * 1024 * 1024

MIX_ROWS = 512
IN_BLOCK = 256
PROJ_AFTER = (0, 2, 3, 5, 6, 8, 9, 11, 12, 14, 17, 20)
MLP_ROWS = 512
FF_BLOCK = 1024

BF16 = jnp.bfloat16
F32 = jnp.float32


def _dot(a, b):
    return jnp.dot(a, b, preferred_element_type=F32)


def _dot_tn(a, b):
    return lax.dot_general(a, b, (((0,), (0,)), ((), ())), preferred_element_type=F32)


def _rms(x, w):
    return x * lax.rsqrt(jnp.mean(x * x, axis=-1, keepdims=True) + EPS) * w


def _group_rms(x, w, groups):
    width = x.shape[-1] // groups
    parts = []
    for g in range(groups):
        sl = slice(g * width, (g + 1) * width)
        parts.append(_rms(x[:, sl], w[:, sl]))
    return jnp.concatenate(parts, axis=-1)


def _gelu(x):
    a = -2.0 * SQRT_2_OVER_PI * LOG2E
    return x / (1.0 + jnp.exp2(x * (a + (a * 0.044715) * (x * x))))


def _level_operands(q, k, b, half):
    rows, width = b.shape
    n = 2 * half
    if half >= SUBLANES:
        shape = (rows // n, 2, half, width)
        b4, q4, k4 = b.reshape(shape), q.reshape(shape), k.reshape(shape)
        first, second = b4[:, 0], b4[:, 1]
        mid = first[:, half - 1:half, :]
        zero = jnp.zeros_like(first)
        qe = jnp.stack([zero, q4[:, 1] * jnp.exp2(second - mid)], axis=1)
        ke = jnp.stack([k4[:, 0] * jnp.exp2(mid - first), zero], axis=1)
        return qe.reshape(rows, width), ke.reshape(rows, width)
    b3 = b.reshape(rows // SUBLANES, SUBLANES, width)
    pos = lax.broadcasted_iota(jnp.int32, b3.shape, 1) % n
    if half == SUBLANES // 2:
        mid = jnp.broadcast_to(b3[:, half - 1:half, :], b3.shape)
    else:
        assert half == 2
        prev1 = pltpu.roll(b3, 1, 1)
        nxt1 = pltpu.roll(b3, SUBLANES - 1, 1)
        prev2 = pltpu.roll(b3, 2, 1)
        mid = jnp.where(pos == 0, nxt1, jnp.where(pos == 1, b3, jnp.where(pos == 2, prev1, prev2)))
    sign = jnp.where(pos < half, -1.0, 1.0)
    e = jnp.exp2((b3 - mid) * sign).reshape(rows, width)
    return q * e, k * e


def _mix_stages(layer, x_ref, z_ref, sw_ref, sbt_ref, nv_ref, nao_ref, lbs_ref, nbo_ref, wout_ref,
                o_ref, st_ref, vn_ref, qs_ref, kst_ref, qe_ref, ke_ref, v_ref, eb_ref, ob_ref,
                y_ref):
    rows = x_ref.shape[0]

    u = _gelu(z_ref[:, 0:D_A])
    yield
    gv = _gelu(z_ref[:, D_A:2 * D_A])
    yield
    vn_ref[...] = _group_rms(gv, nv_ref[...], A_GROUPS).astype(BF16)
    yield
    tri = (lax.broadcasted_iota(jnp.int32, (A_CHUNK, A_CHUNK), 0)
           >= lax.broadcasted_iota(jnp.int32, (A_CHUNK, A_CHUNK), 1))
    for g in range(A_GROUPS):
        w_g = jnp.where(tri, sw_ref[g], 0.0).astype(BF16)
        bias_g = sbt_ref[:, g:g + 1]
        sl = slice(g * HEAD, (g + 1) * HEAD)
        blocks = []
        for c in range(rows // A_CHUNK):
            rs = slice(c * A_CHUNK, (c + 1) * A_CHUNK)
            mixed = _dot(w_g, vn_ref[rs, sl]) + bias_g
            blocks.append(u[rs, sl] * mixed)
        y_g = jnp.concatenate(blocks, axis=0)
        y_ref[:, sl] = _rms(y_g, nao_ref[:, sl]).astype(BF16)
        if g % 2 == 1:
            yield

    base = 2 * D_A
    zf = z_ref[:, base + D_B:base + 2 * D_B]
    v_ref[...] = z_ref[:, base + 2 * D_B:base + 3 * D_B].astype(BF16)

    soft2 = jnp.log2(1.0 + jnp.exp2(-jnp.abs(zf) * LOG2E))
    log2_sig = jnp.minimum(zf, 0.0) * LOG2E - soft2
    yield
    if layer == 0:
        g2 = log2_sig
    else:
        lbs = lbs_ref[...]
        sm = jnp.exp(lbs - jnp.max(lbs, axis=0, keepdims=True))
        sm = sm / jnp.sum(sm, axis=0, keepdims=True)
        lb = jnp.sum(sm[1:layer + 1, :], axis=0, keepdims=True)
        t0 = jnp.log2(lb)
        t1 = jnp.log1p(-lb) * LOG2E + log2_sig
        g2 = jnp.maximum(t0, t1) + jnp.log2(1.0 + jnp.exp2(-jnp.abs(t0 - t1)))
        yield
    f = jnp.exp2(g2)
    kk = 1.0 - f

    g_hi = g2.astype(BF16)
    g_lo = (g2 - g_hi.astype(F32)).astype(BF16)
    ltri = (lax.broadcasted_iota(jnp.int32, (B_CHUNK, B_CHUNK), 0)
            >= lax.broadcasted_iota(jnp.int32, (B_CHUNK, B_CHUNK), 1)).astype(BF16)
    b_parts = []
    for c in range(rows // B_CHUNK):
        rs = slice(c * B_CHUNK, (c + 1) * B_CHUNK)
        b_parts.append(_dot(ltri, g_hi[rs]) + _dot(ltri, g_lo[rs]))
    b = jnp.concatenate(b_parts, axis=0)
    yield

    def put_kt(lvl, k_lvl):
        for hd in range(B_HEADS):
            kst_ref[lvl, hd] = k_lvl[:, hd * HEAD:(hd + 1) * HEAD].T.astype(BF16)

    zq = z_ref[:, base:base + D_B]
    put_kt(0, kk)
    qs_ref[0] = (zq * f).astype(BF16)
    qs_ref[N_LEVELS] = zq.astype(BF16)
    yield
    for lvl in range(1, N_LEVELS):
        q_lvl, k_lvl = _level_operands(zq, kk, b, 1 << lvl)
        qs_ref[lvl] = q_lvl.astype(BF16)
        put_kt(lvl, k_lvl)
        yield

    b3 = b.reshape(rows // B_CHUNK, B_CHUNK, D_B)
    b_last = b3[:, B_CHUNK - 1:B_CHUNK, :]
    qe_ref[...] = (zq * jnp.exp2(b)).astype(BF16)
    ke_ref[...] = (kk.reshape(b3.shape) * jnp.exp2(b_last - b3)).reshape(rows, D_B).astype(BF16)
    eb_ref[...] = jnp.exp2(b_last).reshape(rows // B_CHUNK, D_B)
    yield

    zg = z_ref[:, base + 3 * D_B:base + 4 * D_B]
    gate = zg / (1.0 + jnp.exp2(zg * -LOG2E))
    yield

    ri = lax.broadcasted_iota(jnp.int32, (B_CHUNK, B_CHUNK), 0)
    ci = lax.broadcasted_iota(jnp.int32, (B_CHUNK, B_CHUNK), 1)
    xor = ri ^ ci
    masks = [(ri > ci) & (xor >= (1 << lvl)) & (xor < (2 << lvl)) for lvl in range(N_LEVELS)]
    masks.append(ri == ci)

    for c in range(rows // B_CHUNK):
        rs = slice(c * B_CHUNK, (c + 1) * B_CHUNK)
        decay = eb_ref[c:c + 1, :]
        for hd in range(B_HEADS):
            sl = slice(hd * HEAD, (hd + 1) * HEAD)
            scores = jnp.zeros((B_CHUNK, B_CHUNK), F32)
            for lvl in range(N_LEVELS + 1):
                k_t = kst_ref[0 if lvl == N_LEVELS else lvl, hd, :, rs]
                scores = jnp.where(masks[lvl], _dot(qs_ref[lvl, rs, sl], k_t), scores)
            vh = v_ref[rs, sl]
            st = st_ref[hd]
            o = _dot(scores.astype(BF16), vh) + _dot(qe_ref[rs, sl], st.T.astype(BF16))
            ob_ref[rs, sl] = o
            st_ref[hd] = st * decay[:, sl] + _dot_tn(vh, ke_ref[rs, sl])
            if hd % 2 == 1:
                yield

    y_ref[:, D_A:] = (_group_rms(ob_ref[...], nbo_ref[...], B_HEADS) * gate).astype(BF16)
    yield
    o_ref[...] = x_ref[...] + _dot(y_ref[...], wout_ref[...])


def _proj_stages(x_ref, nmix_ref, win_ref, z_ref):
    h = _rms(x_ref[...], nmix_ref[...]).astype(BF16)
    for j in range(D_IN // IN_BLOCK):
        cs = slice(j * IN_BLOCK, (j + 1) * IN_BLOCK)
        z_ref[:, cs] = _dot(h, win_ref[:, cs])
        yield


def _interleave(main, side, after):
    for i, _ in enumerate(main):
        if i in after:
            next(side, None)
    for _ in side:
        pass


def _mixer_kernel(layer, tiles_per_seq, x_ref, xn_ref, nmix_ref, win_ref, sw_ref, sbt_ref, nv_ref,
                  nao_ref, lbs_ref, nbo_ref, wout_ref, o_ref, zc_ref, zn_ref, st_ref, *work):
    step = pl.program_id(0)

    @pl.when(step % tiles_per_seq == 0)
    def _():
        st_ref[...] = jnp.zeros_like(st_ref)

    @pl.when(step == 0)
    def _():
        for _ in _proj_stages(x_ref, nmix_ref, win_ref, zn_ref):
            pass

    zc_ref[...] = zn_ref[...]
    _interleave(_mix_stages(layer, x_ref, zc_ref, sw_ref, sbt_ref, nv_ref, nao_ref, lbs_ref, nbo_ref,
                            wout_ref, o_ref, st_ref, *work),
                _proj_stages(xn_ref, nmix_ref, win_ref, zn_ref), PROJ_AFTER)


def _mlp_kernel(final, x_ref, nw_ref, wup_ref, wdn_ref, nf_ref, o_ref):
    x = x_ref[...]
    h = _rms(x, nw_ref[...]).astype(BF16)
    acc = x
    for j in range(D_FF // FF_BLOCK):
        cs = slice(j * FF_BLOCK, (j + 1) * FF_BLOCK)
        a = jnp.maximum(_dot(h, wup_ref[:, cs]), 0.0)
        acc = acc + _dot((a * a).astype(BF16), wdn_ref[cs, :])
    if final:
        acc = _rms(acc, nf_ref[...])
    o_ref[...] = acc


def _full(shape):
    return pl.BlockSpec(shape, lambda *_: (0,) * len(shape))


def _mixer(layer, x2, seq, nmix, win, sw, sbt, nv, nao, lbs, nbo, wout):
    n_tok = x2.shape[0]
    rows = MIX_ROWS
    n_tiles = n_tok // rows
    n_chunks = rows // B_CHUNK
    xspec = pl.BlockSpec((rows, D_MODEL), lambda i: (i, 0))
    next_spec = pl.BlockSpec((rows, D_MODEL), lambda i: (jnp.minimum(i + 1, n_tiles - 1), 0))
    return pl.pallas_call(
        functools.partial(_mixer_kernel, layer, seq // rows),
        grid=(n_tiles,),
        in_specs=[xspec, next_spec, _full(nmix.shape), _full(win.shape), _full(sw.shape),
                  _full(sbt.shape), _full(nv.shape), _full(nao.shape), _full(lbs.shape),
                  _full(nbo.shape), _full(wout.shape)],
        out_specs=xspec,
        out_shape=jax.ShapeDtypeStruct(x2.shape, F32),
        scratch_shapes=[
            pltpu.VMEM((rows, D_IN), F32),
            pltpu.VMEM((rows, D_IN), F32),
            pltpu.VMEM((B_HEADS, HEAD, HEAD), F32),
            pltpu.VMEM((rows, D_A), BF16),
            pltpu.VMEM((N_LEVELS + 1, rows, D_B), BF16),
            pltpu.VMEM((N_LEVELS, B_HEADS, HEAD, rows), BF16),
            pltpu.VMEM((rows, D_B), BF16),
            pltpu.VMEM((rows, D_B), BF16),
            pltpu.VMEM((rows, D_B), BF16),
            pltpu.VMEM((n_chunks, D_B), F32),
            pltpu.VMEM((rows, D_B), F32),
            pltpu.VMEM((rows, D_MODEL), BF16),
        ],
        compiler_params=pltpu.CompilerParams(
            dimension_semantics=("arbitrary",), vmem_limit_bytes=VMEM_LIMIT_BYTES),
        name=f"mixer{layer}",
    )(x2, x2, nmix, win, sw, sbt, nv, nao, lbs, nbo, wout)


def _mlp(layer, final, x2, nw, wup, wdn, nf):
    n_tok = x2.shape[0]
    xspec = pl.BlockSpec((MLP_ROWS, D_MODEL), lambda i: (i, 0))
    return pl.pallas_call(
        functools.partial(_mlp_kernel, final),
        grid=(n_tok // MLP_ROWS,),
        in_specs=[xspec, _full(nw.shape), _full(wup.shape), _full(wdn.shape), _full(nf.shape)],
        out_specs=xspec,
        out_shape=jax.ShapeDtypeStruct(x2.shape, F32),
        compiler_params=pltpu.CompilerParams(
            dimension_semantics=("arbitrary",), vmem_limit_bytes=VMEM_LIMIT_BYTES),
        name=f"mlp{layer}",
    )(x2, nw, wup, wdn, nf)


def kernel(x, norm_mix, w_in, spatial_w, spatial_b, norm_v, norm_a_out, lower_bounds, norm_b_out,
           w_out, norm_mlp, w_up, w_down, norm_final):
    bsz, seq, d = x.shape
    depth = w_in.shape[0]
    n_tok = bsz * seq
    assert d == D_MODEL and seq % MIX_ROWS == 0 and n_tok % MLP_ROWS == 0
    row = lambda a: a.reshape(1, -1).astype(F32)
    x2 = x.reshape(n_tok, d)
    for l in range(depth):
        x2 = _mixer(l, x2, seq, row(norm_mix[l]), w_in[l].astype(BF16), spatial_w[l], spatial_b[l].T,
                    row(norm_v[l]), row(norm_a_out[l]), lower_bounds.astype(F32), row(norm_b_out[l]),
                    w_out[l].astype(BF16))
        x2 = _mlp(l, l == depth - 1, x2, row(norm_mlp[l]), w_up[l].astype(BF16),
                  w_down[l].astype(BF16), row(norm_final))
    return x2.reshape(bsz, seq, d)
```

```python
import functools

import jax
import jax.numpy as jnp
from jax import lax
from jax.experimental import pallas as pl
from jax.experimental.pallas import tpu as pltpu

D_MODEL = 1024
D_A = 512
D_B = 512
A_GROUPS = 4
A_CHUNK = 128
B_HEADS = 4
HEAD = 128
B_CHUNK = 128
D_IN = 2 * D_A + 4 * D_B
D_FF = 4 * D_MODEL
EPS = 1e-6
LOG2E = 1.4426950408889634
SQRT_2_OVER_PI = 0.7978845608028654

SUBLANES = 8
N_LEVELS = 7
VMEM_LIMIT_BYTES = 60 * 1024 * 1024

MIX_ROWS = 1024
MLP_ROWS = 1024
FF_BLOCK = 1024

BF16 = jnp.bfloat16
F32 = jnp.float32


def _dot(a, b):
    return jnp.dot(a, b, preferred_element_type=F32)


def _dot_tn(a, b):
    return lax.dot_general(a, b, (((0,), (0,)), ((), ())), preferred_element_type=F32)


def _rms(x, w):
    return x * lax.rsqrt(jnp.mean(x * x, axis=-1, keepdims=True) + EPS) * w


def _group_rms(x, w, groups):
    width = x.shape[-1] // groups
    parts = []
    for g in range(groups):
        sl = slice(g * width, (g + 1) * width)
        parts.append(_rms(x[:, sl], w[:, sl]))
    return jnp.concatenate(parts, axis=-1)


def _gelu(x):
    a = -2.0 * SQRT_2_OVER_PI * LOG2E
    return x / (1.0 + jnp.exp2(x * (a + (a * 0.044715) * (x * x))))


def _level_operands(q, k, b, half):
    rows, width = b.shape
    n = 2 * half
    if half >= SUBLANES:
        shape = (rows // n, 2, half, width)
        b4, q4, k4 = b.reshape(shape), q.reshape(shape), k.reshape(shape)
        first, second = b4[:, 0], b4[:, 1]
        mid = first[:, half - 1:half, :]
        zero = jnp.zeros_like(first)
        qe = jnp.stack([zero, q4[:, 1] * jnp.exp2(second - mid)], axis=1)
        ke = jnp.stack([k4[:, 0] * jnp.exp2(mid - first), zero], axis=1)
        return qe.reshape(rows, width), ke.reshape(rows, width)
    b3 = b.reshape(rows // SUBLANES, SUBLANES, width)
    pos = lax.broadcasted_iota(jnp.int32, b3.shape, 1) % n
    if half == SUBLANES // 2:
        mid = jnp.broadcast_to(b3[:, half - 1:half, :], b3.shape)
    else:
        assert half == 2
        prev1 = pltpu.roll(b3, 1, 1)
        nxt1 = pltpu.roll(b3, SUBLANES - 1, 1)
        prev2 = pltpu.roll(b3, 2, 1)
        mid = jnp.where(pos == 0, nxt1, jnp.where(pos == 1, b3, jnp.where(pos == 2, prev1, prev2)))
    sign = jnp.where(pos < half, -1.0, 1.0)
    e = jnp.exp2((b3 - mid) * sign).reshape(rows, width)
    return q * e, k * e


def _mixer_kernel(layer, x_ref, nmix_ref, win_ref, sw_ref, sbt_ref, nv_ref, nao_ref, lbs_ref,
                  nbo_ref, wout_ref, o_ref,
                  st_ref, z_ref, vn_ref, qs_ref, kst_ref, qe_ref, ke_ref, v_ref, eb_ref, ob_ref, y_ref):
    rows = x_ref.shape[0]

    @pl.when(pl.program_id(1) == 0)
    def _():
        st_ref[...] = jnp.zeros_like(st_ref)

    x = x_ref[...]
    h = _rms(x, nmix_ref[...])
    z_ref[...] = _dot(h.astype(BF16), win_ref[...])

    u = _gelu(z_ref[:, 0:D_A])
    vn_ref[...] = _group_rms(_gelu(z_ref[:, D_A:2 * D_A]), nv_ref[...], A_GROUPS).astype(BF16)
    tri = (lax.broadcasted_iota(jnp.int32, (A_CHUNK, A_CHUNK), 0)
           >= lax.broadcasted_iota(jnp.int32, (A_CHUNK, A_CHUNK), 1))
    ya_parts = []
    for g in range(A_GROUPS):
        w_g = jnp.where(tri, sw_ref[g], 0.0).astype(BF16)
        bias_g = sbt_ref[:, g:g + 1]
        sl = slice(g * HEAD, (g + 1) * HEAD)
        blocks = []
        for c in range(rows // A_CHUNK):
            rs = slice(c * A_CHUNK, (c + 1) * A_CHUNK)
            mixed = _dot(w_g, vn_ref[rs, sl]) + bias_g
            blocks.append(u[rs, sl] * mixed)
        y_g = jnp.concatenate(blocks, axis=0)
        ya_parts.append(_rms(y_g, nao_ref[:, sl]))
    y_ref[:, 0:D_A] = jnp.concatenate(ya_parts, axis=-1).astype(BF16)

    base = 2 * D_A
    zq = z_ref[:, base:base + D_B]
    zf = z_ref[:, base + D_B:base + 2 * D_B]
    v_ref[...] = z_ref[:, base + 2 * D_B:base + 3 * D_B].astype(BF16)

    soft2 = jnp.log2(1.0 + jnp.exp2(-jnp.abs(zf) * LOG2E))
    log2_sig = jnp.minimum(zf, 0.0) * LOG2E - soft2
    if layer == 0:
        g2 = log2_sig
    else:
        lbs = lbs_ref[...]
        sm = jnp.exp(lbs - jnp.max(lbs, axis=0, keepdims=True))
        sm = sm / jnp.sum(sm, axis=0, keepdims=True)
        lb = jnp.sum(sm[1:layer + 1, :], axis=0, keepdims=True)
        t0 = jnp.log2(lb)
        t1 = jnp.log1p(-lb) * LOG2E + log2_sig
        g2 = jnp.maximum(t0, t1) + jnp.log2(1.0 + jnp.exp2(-jnp.abs(t0 - t1)))
    f = jnp.exp2(g2)
    kk = 1.0 - f

    g_hi = g2.astype(BF16)
    g_lo = (g2 - g_hi.astype(F32)).astype(BF16)
    ltri = (lax.broadcasted_iota(jnp.int32, (B_CHUNK, B_CHUNK), 0)
            >= lax.broadcasted_iota(jnp.int32, (B_CHUNK, B_CHUNK), 1)).astype(BF16)
    b_parts = []
    for c in range(rows // B_CHUNK):
        rs = slice(c * B_CHUNK, (c + 1) * B_CHUNK)
        b_parts.append(_dot(ltri, g_hi[rs]) + _dot(ltri, g_lo[rs]))
    b = jnp.concatenate(b_parts, axis=0)

    def put_kt(lvl, k_lvl):
        for hd in range(B_HEADS):
            kst_ref[lvl, hd] = k_lvl[:, hd * HEAD:(hd + 1) * HEAD].T.astype(BF16)

    put_kt(0, kk)
    qs_ref[0] = (zq * f).astype(BF16)
    for lvl in range(1, N_LEVELS):
        q_lvl, k_lvl = _level_operands(zq, kk, b, 1 << lvl)
        qs_ref[lvl] = q_lvl.astype(BF16)
        put_kt(lvl, k_lvl)
    qs_ref[N_LEVELS] = zq.astype(BF16)

    b3 = b.reshape(rows // B_CHUNK, B_CHUNK, D_B)
    b_last = b3[:, B_CHUNK - 1:B_CHUNK, :]
    qe_ref[...] = (zq * jnp.exp2(b)).astype(BF16)
    ke_ref[...] = (kk.reshape(b3.shape) * jnp.exp2(b_last - b3)).reshape(rows, D_B).astype(BF16)
    eb_ref[...] = jnp.exp2(b_last).reshape(rows // B_CHUNK, D_B)

    ri = lax.broadcasted_iota(jnp.int32, (B_CHUNK, B_CHUNK), 0)
    ci = lax.broadcasted_iota(jnp.int32, (B_CHUNK, B_CHUNK), 1)
    xor = ri ^ ci
    masks = [(ri > ci) & (xor >= (1 << lvl)) & (xor < (2 << lvl)) for lvl in range(N_LEVELS)]
    masks.append(ri == ci)

    for c in range(rows // B_CHUNK):
        rs = slice(c * B_CHUNK, (c + 1) * B_CHUNK)
        decay = eb_ref[c:c + 1, :]
        for hd in range(B_HEADS):
            sl = slice(hd * HEAD, (hd + 1) * HEAD)
            scores = jnp.zeros((B_CHUNK, B_CHUNK), F32)
            for lvl in range(N_LEVELS + 1):
                k_t = kst_ref[0 if lvl == N_LEVELS else lvl, hd, :, rs]
                scores = jnp.where(masks[lvl], _dot(qs_ref[lvl, rs, sl], k_t), scores)
            vh = v_ref[rs, sl]
            st = st_ref[hd]
            o = _dot(scores.astype(BF16), vh) + _dot(qe_ref[rs, sl], st.T.astype(BF16))
            ob_ref[rs, sl] = o
            st_ref[hd] = st * decay[:, sl] + _dot_tn(vh, ke_ref[rs, sl])

    zg = z_ref[:, base + 3 * D_B:base + 4 * D_B]
    gate = zg / (1.0 + jnp.exp2(zg * -LOG2E))
    y_ref[:, D_A:] = (_group_rms(ob_ref[...], nbo_ref[...], B_HEADS) * gate).astype(BF16)

    o_ref[...] = x + _dot(y_ref[...], wout_ref[...])


def _mlp_kernel(final, x_ref, nw_ref, wup_ref, wdn_ref, nf_ref, o_ref):
    x = x_ref[...]
    h = _rms(x, nw_ref[...]).astype(BF16)
    acc = x
    for j in range(D_FF // FF_BLOCK):
        cs = slice(j * FF_BLOCK, (j + 1) * FF_BLOCK)
        a = jnp.maximum(_dot(h, wup_ref[:, cs]), 0.0)
        acc = acc + _dot((a * a).astype(BF16), wdn_ref[cs, :])
    if final:
        acc = _rms(acc, nf_ref[...])
    o_ref[...] = acc


def _full(shape):
    return pl.BlockSpec(shape, lambda *_: (0,) * len(shape))


def _mixer(layer, x, nmix, win, sw, sbt, nv, nao, lbs, nbo, wout):
    bsz, seq, _ = x.shape
    rows = MIX_ROWS
    n_chunks = rows // B_CHUNK
    xspec = pl.BlockSpec((None, rows, D_MODEL), lambda bi, si: (bi, si, 0))
    return pl.pallas_call(
        functools.partial(_mixer_kernel, layer),
        grid=(bsz, seq // rows),
        in_specs=[xspec, _full(nmix.shape), _full(win.shape), _full(sw.shape), _full(sbt.shape),
                  _full(nv.shape), _full(nao.shape), _full(lbs.shape), _full(nbo.shape),
                  _full(wout.shape)],
        out_specs=xspec,
        out_shape=jax.ShapeDtypeStruct(x.shape, F32),
        scratch_shapes=[
            pltpu.VMEM((B_HEADS, HEAD, HEAD), F32),
            pltpu.VMEM((rows, D_IN), F32),
            pltpu.VMEM((rows, D_A), BF16),
            pltpu.VMEM((N_LEVELS + 1, rows, D_B), BF16),
            pltpu.VMEM((N_LEVELS, B_HEADS, HEAD, rows), BF16),
            pltpu.VMEM((rows, D_B), BF16),
            pltpu.VMEM((rows, D_B), BF16),
            pltpu.VMEM((rows, D_B), BF16),
            pltpu.VMEM((n_chunks, D_B), F32),
            pltpu.VMEM((rows, D_B), F32),
            pltpu.VMEM((rows, D_MODEL), BF16),
        ],
        compiler_params=pltpu.CompilerParams(
            dimension_semantics=("arbitrary", "arbitrary"), vmem_limit_bytes=VMEM_LIMIT_BYTES),
        name=f"mixer{layer}",
    )(x, nmix, win, sw, sbt, nv, nao, lbs, nbo, wout)


def _mlp(layer, final, x2, nw, wup, wdn, nf):
    n_tok = x2.shape[0]
    xspec = pl.BlockSpec((MLP_ROWS, D_MODEL), lambda i: (i, 0))
    return pl.pallas_call(
        functools.partial(_mlp_kernel, final),
        grid=(n_tok // MLP_ROWS,),
        in_specs=[xspec, _full(nw.shape), _full(wup.shape), _full(wdn.shape), _full(nf.shape)],
        out_specs=xspec,
        out_shape=jax.ShapeDtypeStruct(x2.shape, F32),
        compiler_params=pltpu.CompilerParams(
            dimension_semantics=("arbitrary",), vmem_limit_bytes=VMEM_LIMIT_BYTES),
        name=f"mlp{layer}",
    )(x2, nw, wup, wdn, nf)


def kernel(x, norm_mix, w_in, spatial_w, spatial_b, norm_v, norm_a_out, lower_bounds, norm_b_out,
           w_out, norm_mlp, w_up, w_down, norm_final):
    bsz, seq, d = x.shape
    depth = w_in.shape[0]
    assert d == D_MODEL and seq % MIX_ROWS == 0 and (bsz * seq) % MLP_ROWS == 0
    row = lambda a: a.reshape(1, -1).astype(F32)
    for l in range(depth):
        x = _mixer(l, x, row(norm_mix[l]), w_in[l].astype(BF16), spatial_w[l], spatial_b[l].T,
                   row(norm_v[l]), row(norm_a_out[l]), lower_bounds.astype(F32), row(norm_b_out[l]),
                   w_out[l].astype(BF16))
        x2 = _mlp(l, l == depth - 1, x.reshape(bsz * seq, d), row(norm_mlp[l]),
                  w_up[l].astype(BF16), w_down[l].astype(BF16), row(norm_final))
        x = x2.reshape(bsz, seq, d)
    return x
```

```python
import functools

import jax
import jax.numpy as jnp
from jax import lax
from jax.experimental import pallas as pl
from jax.experimental.pallas import tpu as pltpu

D_MODEL = 1024
D_A = 512
D_B = 512
A_GROUPS = 4
A_CHUNK = 128
B_HEADS = 4
HEAD = 128
B_CHUNK = 128
D_IN = 2 * D_A + 4 * D_B
D_FF = 4 * D_MODEL
EPS = 1e-6
LOG2E = 1.4426950408889634
SQRT_2_OVER_PI = 0.7978845608028654

SUBLANES = 8
N_LEVELS = 7
VMEM_LIMIT_BYTES = 60 * 1024 * 1024

MIX_ROWS = 1024
MLP_ROWS = 1024
FF_BLOCK = 1024
W_CHUNK = 512
assert FF_BLOCK == D_MODEL

BF16 = jnp.bfloat16
F32 = jnp.float32


def _dot(a, b):
    return jnp.dot(a, b, preferred_element_type=F32)


def _dot_tn(a, b):
    return lax.dot_general(a, b, (((0,), (0,)), ((), ())), preferred_element_type=F32)


def _rms(x, w):
    return x * lax.rsqrt(jnp.mean(x * x, axis=-1, keepdims=True) + EPS) * w


def _group_rms(x, w, groups):
    width = x.shape[-1] // groups
    parts = []
    for g in range(groups):
        sl = slice(g * width, (g + 1) * width)
        parts.append(_rms(x[:, sl], w[:, sl]))
    return jnp.concatenate(parts, axis=-1)


def _gelu(x):
    a = -2.0 * SQRT_2_OVER_PI * LOG2E
    return x / (1.0 + jnp.exp2(x * (a + (a * 0.044715) * (x * x))))


def _level_operands(q, k, b, half):
    rows, width = b.shape
    n = 2 * half
    if half >= SUBLANES:
        shape = (rows // n, 2, half, width)
        b4, q4, k4 = b.reshape(shape), q.reshape(shape), k.reshape(shape)
        first, second = b4[:, 0], b4[:, 1]
        mid = first[:, half - 1:half, :]
        zero = jnp.zeros_like(first)
        qe = jnp.stack([zero, q4[:, 1] * jnp.exp2(second - mid)], axis=1)
        ke = jnp.stack([k4[:, 0] * jnp.exp2(mid - first), zero], axis=1)
        return qe.reshape(rows, width), ke.reshape(rows, width)
    b3 = b.reshape(rows // SUBLANES, SUBLANES, width)
    pos = lax.broadcasted_iota(jnp.int32, b3.shape, 1) % n
    if half == SUBLANES // 2:
        mid = jnp.broadcast_to(b3[:, half - 1:half, :], b3.shape)
    else:
        assert half == 2
        prev1 = pltpu.roll(b3, 1, 1)
        nxt1 = pltpu.roll(b3, SUBLANES - 1, 1)
        prev2 = pltpu.roll(b3, 2, 1)
        mid = jnp.where(pos == 0, nxt1, jnp.where(pos == 1, b3, jnp.where(pos == 2, prev1, prev2)))
    sign = jnp.where(pos < half, -1.0, 1.0)
    e = jnp.exp2((b3 - mid) * sign).reshape(rows, width)
    return q * e, k * e


def _mixer_kernel(layer, x_ref, nmix_ref, win_ref, sw_ref, sbt_ref, nv_ref, nao_ref, lbs_ref,
                  nbo_ref, wout_ref, o_ref,
                  st_ref, z_ref, vn_ref, qs_ref, kst_ref, qe_ref, ke_ref, v_ref, eb_ref, ob_ref, y_ref):
    rows = x_ref.shape[0]

    @pl.when(pl.program_id(1) == 0)
    def _():
        st_ref[...] = jnp.zeros_like(st_ref)

    x = x_ref[...]
    h = _rms(x, nmix_ref[...])
    z_ref[...] = _dot(h.astype(BF16), win_ref[...])

    u = _gelu(z_ref[:, 0:D_A])
    vn_ref[...] = _group_rms(_gelu(z_ref[:, D_A:2 * D_A]), nv_ref[...], A_GROUPS).astype(BF16)
    tri = (lax.broadcasted_iota(jnp.int32, (A_CHUNK, A_CHUNK), 0)
           >= lax.broadcasted_iota(jnp.int32, (A_CHUNK, A_CHUNK), 1))
    ya_parts = []
    for g in range(A_GROUPS):
        w_g = jnp.where(tri, sw_ref[g], 0.0).astype(BF16)
        bias_g = sbt_ref[:, g:g + 1]
        sl = slice(g * HEAD, (g + 1) * HEAD)
        blocks = []
        for c in range(rows // A_CHUNK):
            rs = slice(c * A_CHUNK, (c + 1) * A_CHUNK)
            mixed = _dot(w_g, vn_ref[rs, sl]) + bias_g
            blocks.append(u[rs, sl] * mixed)
        y_g = jnp.concatenate(blocks, axis=0)
        ya_parts.append(_rms(y_g, nao_ref[:, sl]))
    y_ref[:, 0:D_A] = jnp.concatenate(ya_parts, axis=-1).astype(BF16)

    base = 2 * D_A
    zq = z_ref[:, base:base + D_B]
    zf = z_ref[:, base + D_B:base + 2 * D_B]
    v_ref[...] = z_ref[:, base + 2 * D_B:base + 3 * D_B].astype(BF16)

    soft2 = jnp.log2(1.0 + jnp.exp2(-jnp.abs(zf) * LOG2E))
    log2_sig = jnp.minimum(zf, 0.0) * LOG2E - soft2
    if layer == 0:
        g2 = log2_sig
    else:
        lbs = lbs_ref[...]
        sm = jnp.exp(lbs - jnp.max(lbs, axis=0, keepdims=True))
        sm = sm / jnp.sum(sm, axis=0, keepdims=True)
        lb = jnp.sum(sm[1:layer + 1, :], axis=0, keepdims=True)
        t0 = jnp.log2(lb)
        t1 = jnp.log1p(-lb) * LOG2E + log2_sig
        g2 = jnp.maximum(t0, t1) + jnp.log2(1.0 + jnp.exp2(-jnp.abs(t0 - t1)))
    f = jnp.exp2(g2)
    kk = 1.0 - f

    g_hi = g2.astype(BF16)
    g_lo = (g2 - g_hi.astype(F32)).astype(BF16)
    ltri = (lax.broadcasted_iota(jnp.int32, (B_CHUNK, B_CHUNK), 0)
            >= lax.broadcasted_iota(jnp.int32, (B_CHUNK, B_CHUNK), 1)).astype(BF16)
    b_parts = []
    for c in range(rows // B_CHUNK):
        rs = slice(c * B_CHUNK, (c + 1) * B_CHUNK)
        b_parts.append(_dot(ltri, g_hi[rs]) + _dot(ltri, g_lo[rs]))
    b = jnp.concatenate(b_parts, axis=0)

    def put_kt(lvl, k_lvl):
        for hd in range(B_HEADS):
            kst_ref[lvl, hd] = k_lvl[:, hd * HEAD:(hd + 1) * HEAD].T.astype(BF16)

    put_kt(0, kk)
    qs_ref[0] = (zq * f).astype(BF16)
    for lvl in range(1, N_LEVELS):
        q_lvl, k_lvl = _level_operands(zq, kk, b, 1 << lvl)
        qs_ref[lvl] = q_lvl.astype(BF16)
        put_kt(lvl, k_lvl)
    qs_ref[N_LEVELS] = zq.astype(BF16)

    b3 = b.reshape(rows // B_CHUNK, B_CHUNK, D_B)
    b_last = b3[:, B_CHUNK - 1:B_CHUNK, :]
    qe_ref[...] = (zq * jnp.exp2(b)).astype(BF16)
    ke_ref[...] = (kk.reshape(b3.shape) * jnp.exp2(b_last - b3)).reshape(rows, D_B).astype(BF16)
    eb_ref[...] = jnp.exp2(b_last).reshape(rows // B_CHUNK, D_B)

    ri = lax.broadcasted_iota(jnp.int32, (B_CHUNK, B_CHUNK), 0)
    ci = lax.broadcasted_iota(jnp.int32, (B_CHUNK, B_CHUNK), 1)
    xor = ri ^ ci
    masks = [(ri > ci) & (xor >= (1 << lvl)) & (xor < (2 << lvl)) for lvl in range(N_LEVELS)]
    masks.append(ri == ci)

    for c in range(rows // B_CHUNK):
        rs = slice(c * B_CHUNK, (c + 1) * B_CHUNK)
        decay = eb_ref[c:c + 1, :]
        for hd in range(B_HEADS):
            sl = slice(hd * HEAD, (hd + 1) * HEAD)
            scores = jnp.zeros((B_CHUNK, B_CHUNK), F32)
            for lvl in range(N_LEVELS + 1):
                k_t = kst_ref[0 if lvl == N_LEVELS else lvl, hd, :, rs]
                scores = jnp.where(masks[lvl], _dot(qs_ref[lvl, rs, sl], k_t), scores)
            vh = v_ref[rs, sl]
            st = st_ref[hd]
            o = _dot(scores.astype(BF16), vh) + _dot(qe_ref[rs, sl], st.T.astype(BF16))
            ob_ref[rs, sl] = o
            st_ref[hd] = st * decay[:, sl] + _dot_tn(vh, ke_ref[rs, sl])

    zg = z_ref[:, base + 3 * D_B:base + 4 * D_B]
    gate = zg / (1.0 + jnp.exp2(zg * -LOG2E))
    y_ref[:, D_A:] = (_group_rms(ob_ref[...], nbo_ref[...], B_HEADS) * gate).astype(BF16)

    o_ref[...] = x + _dot(y_ref[...], wout_ref[...])


def _weight_copies(layer, wup_hbm, wdn_hbm, stage_ref, sem_ref):
    srcs = []
    for j in range(D_FF // FF_BLOCK):
        for p in range(FF_BLOCK // W_CHUNK):
            srcs.append(wup_hbm.at[layer, :, pl.ds(j * FF_BLOCK + p * W_CHUNK, W_CHUNK)])
        for p in range(D_MODEL // W_CHUNK):
            srcs.append(wdn_hbm.at[layer, pl.ds(j * FF_BLOCK, FF_BLOCK), pl.ds(p * W_CHUNK, W_CHUNK)])
    return [pltpu.make_async_copy(src, stage_ref.at[i % 2], sem_ref.at[i % 2])
            for i, src in enumerate(srcs)]


def _mlp_kernel(layer, final, x_ref, nw_ref, nf_ref, wup_hbm, wdn_hbm, o_ref,
                wup_ref, wdn_ref, a_ref, stage_ref, sem_ref):
    x = x_ref[...]
    h = _rms(x, nw_ref[...]).astype(BF16)
    n_blocks = D_FF // FF_BLOCK
    step = pl.program_id(0)

    @pl.when(step == 0)
    def _():
        copies = _weight_copies(layer, wup_hbm, wdn_hbm, stage_ref, sem_ref)

        def fetch(i, dst_ref, rs, cs):
            copies[i].wait()
            dst_ref[rs, cs] = stage_ref[i % 2].astype(BF16)
            if i + 2 < len(copies):
                copies[i + 2].start()

        copies[0].start()
        copies[1].start()
        o_ref[...] = x
        i = 0
        for j in range(n_blocks):
            rows_j = slice(j * FF_BLOCK, (j + 1) * FF_BLOCK)
            for p in range(FF_BLOCK // W_CHUNK):
                cs = slice(j * FF_BLOCK + p * W_CHUNK, j * FF_BLOCK + (p + 1) * W_CHUNK)
                fetch(i, wup_ref, slice(None), cs)
                a = jnp.maximum(_dot(h, wup_ref[:, cs]), 0.0)
                a_ref[:, p * W_CHUNK:(p + 1) * W_CHUNK] = (a * a).astype(BF16)
                i += 1
            for p in range(D_MODEL // W_CHUNK):
                cs = slice(p * W_CHUNK, (p + 1) * W_CHUNK)
                fetch(i, wdn_ref, rows_j, cs)
                o_ref[:, cs] += _dot(a_ref[...], wdn_ref[rows_j, cs])
                i += 1
        if final:
            o_ref[...] = _rms(o_ref[...], nf_ref[...])

    @pl.when(step > 0)
    def _():
        acc = x
        for j in range(n_blocks):
            cs = slice(j * FF_BLOCK, (j + 1) * FF_BLOCK)
            a = jnp.maximum(_dot(h, wup_ref[:, cs]), 0.0)
            acc = acc + _dot((a * a).astype(BF16), wdn_ref[cs, :])
        if final:
            acc = _rms(acc, nf_ref[...])
        o_ref[...] = acc


def _full(shape):
    return pl.BlockSpec(shape, lambda *_: (0,) * len(shape))


def _mixer(layer, x, nmix, win, sw, sbt, nv, nao, lbs, nbo, wout):
    bsz, seq, _ = x.shape
    rows = MIX_ROWS
    n_chunks = rows // B_CHUNK
    xspec = pl.BlockSpec((None, rows, D_MODEL), lambda bi, si: (bi, si, 0))
    return pl.pallas_call(
        functools.partial(_mixer_kernel, layer),
        grid=(bsz, seq // rows),
        in_specs=[xspec, _full(nmix.shape), _full(win.shape), _full(sw.shape), _full(sbt.shape),
                  _full(nv.shape), _full(nao.shape), _full(lbs.shape), _full(nbo.shape),
                  _full(wout.shape)],
        out_specs=xspec,
        out_shape=jax.ShapeDtypeStruct(x.shape, F32),
        scratch_shapes=[
            pltpu.VMEM((B_HEADS, HEAD, HEAD), F32),
            pltpu.VMEM((rows, D_IN), F32),
            pltpu.VMEM((rows, D_A), BF16),
            pltpu.VMEM((N_LEVELS + 1, rows, D_B), BF16),
            pltpu.VMEM((N_LEVELS, B_HEADS, HEAD, rows), BF16),
            pltpu.VMEM((rows, D_B), BF16),
            pltpu.VMEM((rows, D_B), BF16),
            pltpu.VMEM((rows, D_B), BF16),
            pltpu.VMEM((n_chunks, D_B), F32),
            pltpu.VMEM((rows, D_B), F32),
            pltpu.VMEM((rows, D_MODEL), BF16),
        ],
        compiler_params=pltpu.CompilerParams(
            dimension_semantics=("arbitrary", "arbitrary"), vmem_limit_bytes=VMEM_LIMIT_BYTES),
        name=f"mixer{layer}",
    )(x, nmix, win, sw, sbt, nv, nao, lbs, nbo, wout)


def _mlp(layer, final, x2, nw, nf, w_up, w_down):
    n_tok = x2.shape[0]
    xspec = pl.BlockSpec((MLP_ROWS, D_MODEL), lambda i: (i, 0))
    in_hbm = pl.BlockSpec(memory_space=pl.ANY)
    return pl.pallas_call(
        functools.partial(_mlp_kernel, layer, final),
        grid=(n_tok // MLP_ROWS,),
        in_specs=[xspec, _full(nw.shape), _full(nf.shape), in_hbm, in_hbm],
        out_specs=xspec,
        out_shape=jax.ShapeDtypeStruct(x2.shape, F32),
        scratch_shapes=[
            pltpu.VMEM((D_MODEL, D_FF), BF16),
            pltpu.VMEM((D_FF, D_MODEL), BF16),
            pltpu.VMEM((MLP_ROWS, FF_BLOCK), BF16),
            pltpu.VMEM((2, D_MODEL, W_CHUNK), F32),
            pltpu.SemaphoreType.DMA((2,)),
        ],
        compiler_params=pltpu.CompilerParams(
            dimension_semantics=("arbitrary",), vmem_limit_bytes=VMEM_LIMIT_BYTES),
        name=f"mlp{layer}",
    )(x2, nw, nf, w_up, w_down)


def kernel(x, norm_mix, w_in, spatial_w, spatial_b, norm_v, norm_a_out, lower_bounds, norm_b_out,
           w_out, norm_mlp, w_up, w_down, norm_final):
    bsz, seq, d = x.shape
    depth = w_in.shape[0]
    assert d == D_MODEL and seq % MIX_ROWS == 0 and (bsz * seq) % MLP_ROWS == 0
    row = lambda a: a.reshape(1, -1).astype(F32)
    for l in range(depth):
        x = _mixer(l, x, row(norm_mix[l]), w_in[l].astype(BF16), spatial_w[l], spatial_b[l].T,
                   row(norm_v[l]), row(norm_a_out[l]), lower_bounds.astype(F32), row(norm_b_out[l]),
                   w_out[l].astype(BF16))
        x2 = _mlp(l, l == depth - 1, x.reshape(bsz * seq, d), row(norm_mlp[l]), row(norm_final),
                  w_up, w_down)
        x = x2.reshape(bsz, seq, d)
    return x
```

```python
import functools

import jax
import jax.numpy as jnp
from jax import lax
from jax.experimental import pallas as pl
from jax.experimental.pallas import tpu as pltpu

D_MODEL = 1024
D_A = 512
D_B = 512
A_GROUPS = 4
A_CHUNK = 128
B_HEADS = 4
HEAD = 128
B_CHUNK = 128
D_IN = 2 * D_A + 4 * D_B
D_FF = 4 * D_MODEL
EPS = 1e-6
LOG2E = 1.4426950408889634
SQRT_2_OVER_PI = 0.7978845608028654

SUBLANES = 8
N_LEVELS = 7
VMEM_LIMIT_BYTES = 60 * 1024 * 1024

MIX_ROWS = 1024
MLP_ROWS = 1024
FF_BLOCK = 1024
W_CHUNK = 512
W_SLOTS = 4
assert FF_BLOCK == D_MODEL

BF16 = jnp.bfloat16
F32 = jnp.float32


def _dot(a, b):
    return jnp.dot(a, b, preferred_element_type=F32)


def _dot_tn(a, b):
    return lax.dot_general(a, b, (((0,), (0,)), ((), ())), preferred_element_type=F32)


def _rms(x, w):
    return x * lax.rsqrt(jnp.mean(x * x, axis=-1, keepdims=True) + EPS) * w


def _group_rms(x, w, groups):
    width = x.shape[-1] // groups
    parts = []
    for g in range(groups):
        sl = slice(g * width, (g + 1) * width)
        parts.append(_rms(x[:, sl], w[:, sl]))
    return jnp.concatenate(parts, axis=-1)


def _gelu(x):
    a = -2.0 * SQRT_2_OVER_PI * LOG2E
    return x / (1.0 + jnp.exp2(x * (a + (a * 0.044715) * (x * x))))


def _level_operands(q, k, b, half):
    rows, width = b.shape
    n = 2 * half
    if half >= SUBLANES:
        shape = (rows // n, 2, half, width)
        b4, q4, k4 = b.reshape(shape), q.reshape(shape), k.reshape(shape)
        first, second = b4[:, 0], b4[:, 1]
        mid = first[:, half - 1:half, :]
        zero = jnp.zeros_like(first)
        qe = jnp.stack([zero, q4[:, 1] * jnp.exp2(second - mid)], axis=1)
        ke = jnp.stack([k4[:, 0] * jnp.exp2(mid - first), zero], axis=1)
        return qe.reshape(rows, width), ke.reshape(rows, width)
    b3 = b.reshape(rows // SUBLANES, SUBLANES, width)
    pos = lax.broadcasted_iota(jnp.int32, b3.shape, 1) % n
    if half == SUBLANES // 2:
        mid = jnp.broadcast_to(b3[:, half - 1:half, :], b3.shape)
    else:
        assert half == 2
        prev1 = pltpu.roll(b3, 1, 1)
        nxt1 = pltpu.roll(b3, SUBLANES - 1, 1)
        prev2 = pltpu.roll(b3, 2, 1)
        mid = jnp.where(pos == 0, nxt1, jnp.where(pos == 1, b3, jnp.where(pos == 2, prev1, prev2)))
    sign = jnp.where(pos < half, -1.0, 1.0)
    e = jnp.exp2((b3 - mid) * sign).reshape(rows, width)
    return q * e, k * e


def _mixer_kernel(layer, x_ref, nmix_ref, win_ref, sw_ref, sbt_ref, nv_ref, nao_ref, lbs_ref,
                  nbo_ref, wout_ref, o_ref,
                  st_ref, z_ref, vn_ref, qs_ref, kst_ref, qe_ref, ke_ref, v_ref, eb_ref, ob_ref, y_ref):
    rows = x_ref.shape[0]

    @pl.when(pl.program_id(1) == 0)
    def _():
        st_ref[...] = jnp.zeros_like(st_ref)

    x = x_ref[...]
    h = _rms(x, nmix_ref[...])
    z_ref[...] = _dot(h.astype(BF16), win_ref[...])

    u = _gelu(z_ref[:, 0:D_A])
    vn_ref[...] = _group_rms(_gelu(z_ref[:, D_A:2 * D_A]), nv_ref[...], A_GROUPS).astype(BF16)
    tri = (lax.broadcasted_iota(jnp.int32, (A_CHUNK, A_CHUNK), 0)
           >= lax.broadcasted_iota(jnp.int32, (A_CHUNK, A_CHUNK), 1))
    ya_parts = []
    for g in range(A_GROUPS):
        w_g = jnp.where(tri, sw_ref[g], 0.0).astype(BF16)
        bias_g = sbt_ref[:, g:g + 1]
        sl = slice(g * HEAD, (g + 1) * HEAD)
        blocks = []
        for c in range(rows // A_CHUNK):
            rs = slice(c * A_CHUNK, (c + 1) * A_CHUNK)
            mixed = _dot(w_g, vn_ref[rs, sl]) + bias_g
            blocks.append(u[rs, sl] * mixed)
        y_g = jnp.concatenate(blocks, axis=0)
        ya_parts.append(_rms(y_g, nao_ref[:, sl]))
    y_ref[:, 0:D_A] = jnp.concatenate(ya_parts, axis=-1).astype(BF16)

    base = 2 * D_A
    zq = z_ref[:, base:base + D_B]
    zf = z_ref[:, base + D_B:base + 2 * D_B]
    v_ref[...] = z_ref[:, base + 2 * D_B:base + 3 * D_B].astype(BF16)

    soft2 = jnp.log2(1.0 + jnp.exp2(-jnp.abs(zf) * LOG2E))
    log2_sig = jnp.minimum(zf, 0.0) * LOG2E - soft2
    if layer == 0:
        g2 = log2_sig
    else:
        lbs = lbs_ref[...]
        sm = jnp.exp(lbs - jnp.max(lbs, axis=0, keepdims=True))
        sm = sm / jnp.sum(sm, axis=0, keepdims=True)
        lb = jnp.sum(sm[1:layer + 1, :], axis=0, keepdims=True)
        t0 = jnp.log2(lb)
        t1 = jnp.log1p(-lb) * LOG2E + log2_sig
        g2 = jnp.maximum(t0, t1) + jnp.log2(1.0 + jnp.exp2(-jnp.abs(t0 - t1)))
    f = jnp.exp2(g2)
    kk = 1.0 - f

    g_hi = g2.astype(BF16)
    g_lo = (g2 - g_hi.astype(F32)).astype(BF16)
    ltri = (lax.broadcasted_iota(jnp.int32, (B_CHUNK, B_CHUNK), 0)
            >= lax.broadcasted_iota(jnp.int32, (B_CHUNK, B_CHUNK), 1)).astype(BF16)
    b_parts = []
    for c in range(rows // B_CHUNK):
        rs = slice(c * B_CHUNK, (c + 1) * B_CHUNK)
        b_parts.append(_dot(ltri, g_hi[rs]) + _dot(ltri, g_lo[rs]))
    b = jnp.concatenate(b_parts, axis=0)

    def put_kt(lvl, k_lvl):
        for hd in range(B_HEADS):
            kst_ref[lvl, hd] = k_lvl[:, hd * HEAD:(hd + 1) * HEAD].T.astype(BF16)

    put_kt(0, kk)
    qs_ref[0] = (zq * f).astype(BF16)
    for lvl in range(1, N_LEVELS):
        q_lvl, k_lvl = _level_operands(zq, kk, b, 1 << lvl)
        qs_ref[lvl] = q_lvl.astype(BF16)
        put_kt(lvl, k_lvl)
    qs_ref[N_LEVELS] = zq.astype(BF16)

    b3 = b.reshape(rows // B_CHUNK, B_CHUNK, D_B)
    b_last = b3[:, B_CHUNK - 1:B_CHUNK, :]
    qe_ref[...] = (zq * jnp.exp2(b)).astype(BF16)
    ke_ref[...] = (kk.reshape(b3.shape) * jnp.exp2(b_last - b3)).reshape(rows, D_B).astype(BF16)
    eb_ref[...] = jnp.exp2(b_last).reshape(rows // B_CHUNK, D_B)

    ri = lax.broadcasted_iota(jnp.int32, (B_CHUNK, B_CHUNK), 0)
    ci = lax.broadcasted_iota(jnp.int32, (B_CHUNK, B_CHUNK), 1)
    xor = ri ^ ci
    masks = [(ri > ci) & (xor >= (1 << lvl)) & (xor < (2 << lvl)) for lvl in range(N_LEVELS)]
    masks.append(ri == ci)

    for c in range(rows // B_CHUNK):
        rs = slice(c * B_CHUNK, (c + 1) * B_CHUNK)
        decay = eb_ref[c:c + 1, :]
        for hd in range(B_HEADS):
            sl = slice(hd * HEAD, (hd + 1) * HEAD)
            scores = jnp.zeros((B_CHUNK, B_CHUNK), F32)
            for lvl in range(N_LEVELS + 1):
                k_t = kst_ref[0 if lvl == N_LEVELS else lvl, hd, :, rs]
                scores = jnp.where(masks[lvl], _dot(qs_ref[lvl, rs, sl], k_t), scores)
            vh = v_ref[rs, sl]
            st = st_ref[hd]
            o = _dot(scores.astype(BF16), vh) + _dot(qe_ref[rs, sl], st.T.astype(BF16))
            ob_ref[rs, sl] = o
            st_ref[hd] = st * decay[:, sl] + _dot_tn(vh, ke_ref[rs, sl])

    zg = z_ref[:, base + 3 * D_B:base + 4 * D_B]
    gate = zg / (1.0 + jnp.exp2(zg * -LOG2E))
    y_ref[:, D_A:] = (_group_rms(ob_ref[...], nbo_ref[...], B_HEADS) * gate).astype(BF16)

    o_ref[...] = x + _dot(y_ref[...], wout_ref[...])


def _weight_copies(layer, wup_hbm, wdn_hbm, stage_ref, sem_ref):
    srcs = []
    for j in range(D_FF // FF_BLOCK):
        for p in range(FF_BLOCK // W_CHUNK):
            srcs.append(wup_hbm.at[layer, :, pl.ds(j * FF_BLOCK + p * W_CHUNK, W_CHUNK)])
        for p in range(D_MODEL // W_CHUNK):
            srcs.append(wdn_hbm.at[layer, pl.ds(j * FF_BLOCK, FF_BLOCK), pl.ds(p * W_CHUNK, W_CHUNK)])
    return [pltpu.make_async_copy(src, stage_ref.at[i % W_SLOTS], sem_ref.at[i % W_SLOTS])
            for i, src in enumerate(srcs)]


def _mlp_kernel(layer, final, x_ref, nw_ref, nf_ref, wup_hbm, wdn_hbm, o_ref,
                wup_ref, wdn_ref, a_ref, stage_ref, sem_ref):
    x = x_ref[...]
    h = _rms(x, nw_ref[...]).astype(BF16)
    n_blocks = D_FF // FF_BLOCK
    step = pl.program_id(0)

    @pl.when(step == 0)
    def _():
        copies = _weight_copies(layer, wup_hbm, wdn_hbm, stage_ref, sem_ref)

        def fetch(i, dst_ref, rs, cs):
            copies[i].wait()
            dst_ref[rs, cs] = stage_ref[i % W_SLOTS].astype(BF16)
            if i + W_SLOTS < len(copies):
                copies[i + W_SLOTS].start()

        for c in copies[:W_SLOTS]:
            c.start()
        o_ref[...] = x
        i = 0
        for j in range(n_blocks):
            rows_j = slice(j * FF_BLOCK, (j + 1) * FF_BLOCK)
            for p in range(FF_BLOCK // W_CHUNK):
                cs = slice(j * FF_BLOCK + p * W_CHUNK, j * FF_BLOCK + (p + 1) * W_CHUNK)
                fetch(i, wup_ref, slice(None), cs)
                a = jnp.maximum(_dot(h, wup_ref[:, cs]), 0.0)
                a_ref[:, p * W_CHUNK:(p + 1) * W_CHUNK] = (a * a).astype(BF16)
                i += 1
            for p in range(D_MODEL // W_CHUNK):
                cs = slice(p * W_CHUNK, (p + 1) * W_CHUNK)
                fetch(i, wdn_ref, rows_j, cs)
                o_ref[:, cs] += _dot(a_ref[...], wdn_ref[rows_j, cs])
                i += 1
        if final:
            o_ref[...] = _rms(o_ref[...], nf_ref[...])

    @pl.when(step > 0)
    def _():
        acc = x
        for j in range(n_blocks):
            cs = slice(j * FF_BLOCK, (j + 1) * FF_BLOCK)
            a = jnp.maximum(_dot(h, wup_ref[:, cs]), 0.0)
            acc = acc + _dot((a * a).astype(BF16), wdn_ref[cs, :])
        if final:
            acc = _rms(acc, nf_ref[...])
        o_ref[...] = acc


def _full(shape):
    return pl.BlockSpec(shape, lambda *_: (0,) * len(shape))


def _mixer(layer, x, nmix, win, sw, sbt, nv, nao, lbs, nbo, wout):
    bsz, seq, _ = x.shape
    rows = MIX_ROWS
    n_chunks = rows // B_CHUNK
    xspec = pl.BlockSpec((None, rows, D_MODEL), lambda bi, si: (bi, si, 0))
    return pl.pallas_call(
        functools.partial(_mixer_kernel, layer),
        grid=(bsz, seq // rows),
        in_specs=[xspec, _full(nmix.shape), _full(win.shape), _full(sw.shape), _full(sbt.shape),
                  _full(nv.shape), _full(nao.shape), _full(lbs.shape), _full(nbo.shape),
                  _full(wout.shape)],
        out_specs=xspec,
        out_shape=jax.ShapeDtypeStruct(x.shape, F32),
        scratch_shapes=[
            pltpu.VMEM((B_HEADS, HEAD, HEAD), F32),
            pltpu.VMEM((rows, D_IN), F32),
            pltpu.VMEM((rows, D_A), BF16),
            pltpu.VMEM((N_LEVELS + 1, rows, D_B), BF16),
            pltpu.VMEM((N_LEVELS, B_HEADS, HEAD, rows), BF16),
            pltpu.VMEM((rows, D_B), BF16),
            pltpu.VMEM((rows, D_B), BF16),
            pltpu.VMEM((rows, D_B), BF16),
            pltpu.VMEM((n_chunks, D_B), F32),
            pltpu.VMEM((rows, D_B), F32),
            pltpu.VMEM((rows, D_MODEL), BF16),
        ],
        compiler_params=pltpu.CompilerParams(
            dimension_semantics=("arbitrary", "arbitrary"), vmem_limit_bytes=VMEM_LIMIT_BYTES),
        name=f"mixer{layer}",
    )(x, nmix, win, sw, sbt, nv, nao, lbs, nbo, wout)


def _mlp(layer, final, x2, nw, nf, w_up, w_down):
    n_tok = x2.shape[0]
    xspec = pl.BlockSpec((MLP_ROWS, D_MODEL), lambda i: (i, 0))
    in_hbm = pl.BlockSpec(memory_space=pl.ANY)
    return pl.pallas_call(
        functools.partial(_mlp_kernel, layer, final),
        grid=(n_tok // MLP_ROWS,),
        in_specs=[xspec, _full(nw.shape), _full(nf.shape), in_hbm, in_hbm],
        out_specs=xspec,
        out_shape=jax.ShapeDtypeStruct(x2.shape, F32),
        scratch_shapes=[
            pltpu.VMEM((D_MODEL, D_FF), BF16),
            pltpu.VMEM((D_FF, D_MODEL), BF16),
            pltpu.VMEM((MLP_ROWS, FF_BLOCK), BF16),
            pltpu.VMEM((W_SLOTS, D_MODEL, W_CHUNK), F32),
            pltpu.SemaphoreType.DMA((W_SLOTS,)),
        ],
        compiler_params=pltpu.CompilerParams(
            dimension_semantics=("arbitrary",), vmem_limit_bytes=VMEM_LIMIT_BYTES),
        name=f"mlp{layer}",
    )(x2, nw, nf, w_up, w_down)


def kernel(x, norm_mix, w_in, spatial_w, spatial_b, norm_v, norm_a_out, lower_bounds, norm_b_out,
           w_out, norm_mlp, w_up, w_down, norm_final):
    bsz, seq, d = x.shape
    depth = w_in.shape[0]
    assert d == D_MODEL and seq % MIX_ROWS == 0 and (bsz * seq) % MLP_ROWS == 0
    row = lambda a: a.reshape(1, -1).astype(F32)
    for l in range(depth):
        x = _mixer(l, x, row(norm_mix[l]), w_in[l].astype(BF16), spatial_w[l], spatial_b[l].T,
                   row(norm_v[l]), row(norm_a_out[l]), lower_bounds.astype(F32), row(norm_b_out[l]),
                   w_out[l].astype(BF16))
        x2 = _mlp(l, l == depth - 1, x.reshape(bsz * seq, d), row(norm_mlp[l]), row(norm_final),
                  w_up, w_down)
        x = x2.reshape(bsz, seq, d)
    return x
```

```python
import functools

import jax
import jax.numpy as jnp
from jax import lax
from jax.experimental import pallas as pl
from jax.experimental.pallas import tpu as pltpu

D_MODEL = 1024
D_A = 512
D_B = 512
A_GROUPS = 4
A_CHUNK = 128
B_HEADS = 4
HEAD = 128
B_CHUNK = 128
D_IN = 2 * D_A + 4 * D_B
D_FF = 4 * D_MODEL
EPS = 1e-6
LOG2E = 1.4426950408889634
SQRT_2_OVER_PI = 0.7978845608028654

SUBLANES = 8
N_LEVELS = 7
VMEM_LIMIT_BYTES = 60 * 1024 * 1024

MIX_ROWS = 1024
MLP_ROWS = 1024
FF_BLOCK = 1024
W_CHUNK = 1024
W_SLOTS = 2
assert FF_BLOCK == D_MODEL

BF16 = jnp.bfloat16
F32 = jnp.float32


def _dot(a, b):
    return jnp.dot(a, b, preferred_element_type=F32)


def _dot_tn(a, b):
    return lax.dot_general(a, b, (((0,), (0,)), ((), ())), preferred_element_type=F32)


def _rms(x, w):
    return x * lax.rsqrt(jnp.mean(x * x, axis=-1, keepdims=True) + EPS) * w


def _group_rms(x, w, groups):
    width = x.shape[-1] // groups
    parts = []
    for g in range(groups):
        sl = slice(g * width, (g + 1) * width)
        parts.append(_rms(x[:, sl], w[:, sl]))
    return jnp.concatenate(parts, axis=-1)


def _gelu(x):
    a = -2.0 * SQRT_2_OVER_PI * LOG2E
    return x / (1.0 + jnp.exp2(x * (a + (a * 0.044715) * (x * x))))


def _level_operands(q, k, b, half):
    rows, width = b.shape
    n = 2 * half
    if half >= SUBLANES:
        shape = (rows // n, 2, half, width)
        b4, q4, k4 = b.reshape(shape), q.reshape(shape), k.reshape(shape)
        first, second = b4[:, 0], b4[:, 1]
        mid = first[:, half - 1:half, :]
        zero = jnp.zeros_like(first)
        qe = jnp.stack([zero, q4[:, 1] * jnp.exp2(second - mid)], axis=1)
        ke = jnp.stack([k4[:, 0] * jnp.exp2(mid - first), zero], axis=1)
        return qe.reshape(rows, width), ke.reshape(rows, width)
    b3 = b.reshape(rows // SUBLANES, SUBLANES, width)
    pos = lax.broadcasted_iota(jnp.int32, b3.shape, 1) % n
    if half == SUBLANES // 2:
        mid = jnp.broadcast_to(b3[:, half - 1:half, :], b3.shape)
    else:
        assert half == 2
        prev1 = pltpu.roll(b3, 1, 1)
        nxt1 = pltpu.roll(b3, SUBLANES - 1, 1)
        prev2 = pltpu.roll(b3, 2, 1)
        mid = jnp.where(pos == 0, nxt1, jnp.where(pos == 1, b3, jnp.where(pos == 2, prev1, prev2)))
    sign = jnp.where(pos < half, -1.0, 1.0)
    e = jnp.exp2((b3 - mid) * sign).reshape(rows, width)
    return q * e, k * e


def _mixer_kernel(layer, x_ref, nmix_ref, win_ref, sw_ref, sbt_ref, nv_ref, nao_ref, lbs_ref,
                  nbo_ref, wout_ref, o_ref,
                  st_ref, z_ref, vn_ref, qs_ref, kst_ref, qe_ref, ke_ref, v_ref, eb_ref, ob_ref, y_ref):
    rows = x_ref.shape[0]

    @pl.when(pl.program_id(1) == 0)
    def _():
        st_ref[...] = jnp.zeros_like(st_ref)

    x = x_ref[...]
    h = _rms(x, nmix_ref[...])
    z_ref[...] = _dot(h.astype(BF16), win_ref[...])

    u = _gelu(z_ref[:, 0:D_A])
    vn_ref[...] = _group_rms(_gelu(z_ref[:, D_A:2 * D_A]), nv_ref[...], A_GROUPS).astype(BF16)
    tri = (lax.broadcasted_iota(jnp.int32, (A_CHUNK, A_CHUNK), 0)
           >= lax.broadcasted_iota(jnp.int32, (A_CHUNK, A_CHUNK), 1))
    ya_parts = []
    for g in range(A_GROUPS):
        w_g = jnp.where(tri, sw_ref[g], 0.0).astype(BF16)
        bias_g = sbt_ref[:, g:g + 1]
        sl = slice(g * HEAD, (g + 1) * HEAD)
        blocks = []
        for c in range(rows // A_CHUNK):
            rs = slice(c * A_CHUNK, (c + 1) * A_CHUNK)
            mixed = _dot(w_g, vn_ref[rs, sl]) + bias_g
            blocks.append(u[rs, sl] * mixed)
        y_g = jnp.concatenate(blocks, axis=0)
        ya_parts.append(_rms(y_g, nao_ref[:, sl]))
    y_ref[:, 0:D_A] = jnp.concatenate(ya_parts, axis=-1).astype(BF16)

    base = 2 * D_A
    zq = z_ref[:, base:base + D_B]
    zf = z_ref[:, base + D_B:base + 2 * D_B]
    v_ref[...] = z_ref[:, base + 2 * D_B:base + 3 * D_B].astype(BF16)

    soft2 = jnp.log2(1.0 + jnp.exp2(-jnp.abs(zf) * LOG2E))
    log2_sig = jnp.minimum(zf, 0.0) * LOG2E - soft2
    if layer == 0:
        g2 = log2_sig
    else:
        lbs = lbs_ref[...]
        sm = jnp.exp(lbs - jnp.max(lbs, axis=0, keepdims=True))
        sm = sm / jnp.sum(sm, axis=0, keepdims=True)
        lb = jnp.sum(sm[1:layer + 1, :], axis=0, keepdims=True)
        t0 = jnp.log2(lb)
        t1 = jnp.log1p(-lb) * LOG2E + log2_sig
        g2 = jnp.maximum(t0, t1) + jnp.log2(1.0 + jnp.exp2(-jnp.abs(t0 - t1)))
    f = jnp.exp2(g2)
    kk = 1.0 - f

    g_hi = g2.astype(BF16)
    g_lo = (g2 - g_hi.astype(F32)).astype(BF16)
    ltri = (lax.broadcasted_iota(jnp.int32, (B_CHUNK, B_CHUNK), 0)
            >= lax.broadcasted_iota(jnp.int32, (B_CHUNK, B_CHUNK), 1)).astype(BF16)
    b_parts = []
    for c in range(rows // B_CHUNK):
        rs = slice(c * B_CHUNK, (c + 1) * B_CHUNK)
        b_parts.append(_dot(ltri, g_hi[rs]) + _dot(ltri, g_lo[rs]))
    b = jnp.concatenate(b_parts, axis=0)

    def put_kt(lvl, k_lvl):
        for hd in range(B_HEADS):
            kst_ref[lvl, hd] = k_lvl[:, hd * HEAD:(hd + 1) * HEAD].T.astype(BF16)

    put_kt(0, kk)
    qs_ref[0] = (zq * f).astype(BF16)
    for lvl in range(1, N_LEVELS):
        q_lvl, k_lvl = _level_operands(zq, kk, b, 1 << lvl)
        qs_ref[lvl] = q_lvl.astype(BF16)
        put_kt(lvl, k_lvl)
    qs_ref[N_LEVELS] = zq.astype(BF16)

    b3 = b.reshape(rows // B_CHUNK, B_CHUNK, D_B)
    b_last = b3[:, B_CHUNK - 1:B_CHUNK, :]
    qe_ref[...] = (zq * jnp.exp2(b)).astype(BF16)
    ke_ref[...] = (kk.reshape(b3.shape) * jnp.exp2(b_last - b3)).reshape(rows, D_B).astype(BF16)
    eb_ref[...] = jnp.exp2(b_last).reshape(rows // B_CHUNK, D_B)

    ri = lax.broadcasted_iota(jnp.int32, (B_CHUNK, B_CHUNK), 0)
    ci = lax.broadcasted_iota(jnp.int32, (B_CHUNK, B_CHUNK), 1)
    xor = ri ^ ci
    masks = [(ri > ci) & (xor >= (1 << lvl)) & (xor < (2 << lvl)) for lvl in range(N_LEVELS)]
    masks.append(ri == ci)

    for c in range(rows // B_CHUNK):
        rs = slice(c * B_CHUNK, (c + 1) * B_CHUNK)
        decay = eb_ref[c:c + 1, :]
        for hd in range(B_HEADS):
            sl = slice(hd * HEAD, (hd + 1) * HEAD)
            scores = jnp.zeros((B_CHUNK, B_CHUNK), F32)
            for lvl in range(N_LEVELS + 1):
                k_t = kst_ref[0 if lvl == N_LEVELS else lvl, hd, :, rs]
                scores = jnp.where(masks[lvl], _dot(qs_ref[lvl, rs, sl], k_t), scores)
            vh = v_ref[rs, sl]
            st = st_ref[hd]
            o = _dot(scores.astype(BF16), vh) + _dot(qe_ref[rs, sl], st.T.astype(BF16))
            ob_ref[rs, sl] = o
            st_ref[hd] = st * decay[:, sl] + _dot_tn(vh, ke_ref[rs, sl])

    zg = z_ref[:, base + 3 * D_B:base + 4 * D_B]
    gate = zg / (1.0 + jnp.exp2(zg * -LOG2E))
    y_ref[:, D_A:] = (_group_rms(ob_ref[...], nbo_ref[...], B_HEADS) * gate).astype(BF16)

    o_ref[...] = x + _dot(y_ref[...], wout_ref[...])


def _weight_copies(layer, wup_hbm, wdn_hbm, stage_ref, sem_ref):
    srcs = []
    for j in range(D_FF // FF_BLOCK):
        for p in range(FF_BLOCK // W_CHUNK):
            srcs.append(wup_hbm.at[layer, :, pl.ds(j * FF_BLOCK + p * W_CHUNK, W_CHUNK)])
        for p in range(D_MODEL // W_CHUNK):
            srcs.append(wdn_hbm.at[layer, pl.ds(j * FF_BLOCK, FF_BLOCK), pl.ds(p * W_CHUNK, W_CHUNK)])
    return [pltpu.make_async_copy(src, stage_ref.at[i % W_SLOTS], sem_ref.at[i % W_SLOTS])
            for i, src in enumerate(srcs)]


def _mlp_kernel(layer, final, x_ref, nw_ref, nf_ref, wup_hbm, wdn_hbm, o_ref,
                wup_ref, wdn_ref, a_ref, stage_ref, sem_ref):
    x = x_ref[...]
    h = _rms(x, nw_ref[...]).astype(BF16)
    n_blocks = D_FF // FF_BLOCK
    step = pl.program_id(0)

    @pl.when(step == 0)
    def _():
        copies = _weight_copies(layer, wup_hbm, wdn_hbm, stage_ref, sem_ref)

        def fetch(i, dst_ref, rs, cs):
            copies[i].wait()
            dst_ref[rs, cs] = stage_ref[i % W_SLOTS].astype(BF16)
            if i + W_SLOTS < len(copies):
                copies[i + W_SLOTS].start()

        for c in copies[:W_SLOTS]:
            c.start()
        o_ref[...] = x
        i = 0
        for j in range(n_blocks):
            rows_j = slice(j * FF_BLOCK, (j + 1) * FF_BLOCK)
            for p in range(FF_BLOCK // W_CHUNK):
                cs = slice(j * FF_BLOCK + p * W_CHUNK, j * FF_BLOCK + (p + 1) * W_CHUNK)
                fetch(i, wup_ref, slice(None), cs)
                a = jnp.maximum(_dot(h, wup_ref[:, cs]), 0.0)
                a_ref[:, p * W_CHUNK:(p + 1) * W_CHUNK] = (a * a).astype(BF16)
                i += 1
            for p in range(D_MODEL // W_CHUNK):
                cs = slice(p * W_CHUNK, (p + 1) * W_CHUNK)
                fetch(i, wdn_ref, rows_j, cs)
                o_ref[:, cs] += _dot(a_ref[...], wdn_ref[rows_j, cs])
                i += 1
        if final:
            o_ref[...] = _rms(o_ref[...], nf_ref[...])

    @pl.when(step > 0)
    def _():
        acc = x
        for j in range(n_blocks):
            cs = slice(j * FF_BLOCK, (j + 1) * FF_BLOCK)
            a = jnp.maximum(_dot(h, wup_ref[:, cs]), 0.0)
            acc = acc + _dot((a * a).astype(BF16), wdn_ref[cs, :])
        if final:
            acc = _rms(acc, nf_ref[...])
        o_ref[...] = acc


def _full(shape):
    return pl.BlockSpec(shape, lambda *_: (0,) * len(shape))


def _mixer(layer, x, nmix, win, sw, sbt, nv, nao, lbs, nbo, wout):
    bsz, seq, _ = x.shape
    rows = MIX_ROWS
    n_chunks = rows // B_CHUNK
    xspec = pl.BlockSpec((None, rows, D_MODEL), lambda bi, si: (bi, si, 0))
    return pl.pallas_call(
        functools.partial(_mixer_kernel, layer),
        grid=(bsz, seq // rows),
        in_specs=[xspec, _full(nmix.shape), _full(win.shape), _full(sw.shape), _full(sbt.shape),
                  _full(nv.shape), _full(nao.shape), _full(lbs.shape), _full(nbo.shape),
                  _full(wout.shape)],
        out_specs=xspec,
        out_shape=jax.ShapeDtypeStruct(x.shape, F32),
        scratch_shapes=[
            pltpu.VMEM((B_HEADS, HEAD, HEAD), F32),
            pltpu.VMEM((rows, D_IN), F32),
            pltpu.VMEM((rows, D_A), BF16),
            pltpu.VMEM((N_LEVELS + 1, rows, D_B), BF16),
            pltpu.VMEM((N_LEVELS, B_HEADS, HEAD, rows), BF16),
            pltpu.VMEM((rows, D_B), BF16),
            pltpu.VMEM((rows, D_B), BF16),
            pltpu.VMEM((rows, D_B), BF16),
            pltpu.VMEM((n_chunks, D_B), F32),
            pltpu.VMEM((rows, D_B), F32),
            pltpu.VMEM((rows, D_MODEL), BF16),
        ],
        compiler_params=pltpu.CompilerParams(
            dimension_semantics=("arbitrary", "arbitrary"), vmem_limit_bytes=VMEM_LIMIT_BYTES),
        name=f"mixer{layer}",
    )(x, nmix, win, sw, sbt, nv, nao, lbs, nbo, wout)


def _mlp(layer, final, x2, nw, nf, w_up, w_down):
    n_tok = x2.shape[0]
    xspec = pl.BlockSpec((MLP_ROWS, D_MODEL), lambda i: (i, 0))
    in_hbm = pl.BlockSpec(memory_space=pl.ANY)
    return pl.pallas_call(
        functools.partial(_mlp_kernel, layer, final),
        grid=(n_tok // MLP_ROWS,),
        in_specs=[xspec, _full(nw.shape), _full(nf.shape), in_hbm, in_hbm],
        out_specs=xspec,
        out_shape=jax.ShapeDtypeStruct(x2.shape, F32),
        scratch_shapes=[
            pltpu.VMEM((D_MODEL, D_FF), BF16),
            pltpu.VMEM((D_FF, D_MODEL), BF16),
            pltpu.VMEM((MLP_ROWS, FF_BLOCK), BF16),
            pltpu.VMEM((W_SLOTS, D_MODEL, W_CHUNK), F32),
            pltpu.SemaphoreType.DMA((W_SLOTS,)),
        ],
        compiler_params=pltpu.CompilerParams(
            dimension_semantics=("arbitrary",), vmem_limit_bytes=VMEM_LIMIT_BYTES),
        name=f"mlp{layer}",
    )(x2, nw, nf, w_up, w_down)


def kernel(x, norm_mix, w_in, spatial_w, spatial_b, norm_v, norm_a_out, lower_bounds, norm_b_out,
           w_out, norm_mlp, w_up, w_down, norm_final):
    bsz, seq, d = x.shape
    depth = w_in.shape[0]
    assert d == D_MODEL and seq % MIX_ROWS == 0 and (bsz * seq) % MLP_ROWS == 0
    row = lambda a: a.reshape(1, -1).astype(F32)
    for l in range(depth):
        x = _mixer(l, x, row(norm_mix[l]), w_in[l].astype(BF16), spatial_w[l], spatial_b[l].T,
                   row(norm_v[l]), row(norm_a_out[l]), lower_bounds.astype(F32), row(norm_b_out[l]),
                   w_out[l].astype(BF16))
        x2 = _mlp(l, l == depth - 1, x.reshape(bsz * seq, d), row(norm_mlp[l]), row(norm_final),
                  w_up, w_down)
        x = x2.reshape(bsz, seq, d)
    return x
```

```python
import functools

import jax
import jax.numpy as jnp
from jax import lax
from jax.experimental import pallas as pl
from jax.experimental.pallas import tpu as pltpu

D_MODEL = 1024
D_A = 512
D_B = 512
A_GROUPS = 4
A_CHUNK = 128
B_HEADS = 4
HEAD = 128
B_CHUNK = 128
D_IN = 2 * D_A + 4 * D_B
D_FF = 4 * D_MODEL
EPS = 1e-6
LOG2E = 1.4426950408889634
SQRT_2_OVER_PI = 0.7978845608028654

SUBLANES = 8
N_LEVELS = 7
VMEM_LIMIT_BYTES = 60 * 1024 * 1024

MIX_ROWS = 1024
MLP_ROWS = 1024
FF_BLOCK = 1024

BF16 = jnp.bfloat16
F32 = jnp.float32


def _dot(a, b):
    return jnp.dot(a, b, preferred_element_type=F32)


def _dot_tn(a, b):
    return lax.dot_general(a, b, (((0,), (0,)), ((), ())), preferred_element_type=F32)


def _rms(x, w):
    return x * lax.rsqrt(jnp.mean(x * x, axis=-1, keepdims=True) + EPS) * w


def _group_rms(x, w, groups):
    width = x.shape[-1] // groups
    parts = []
    for g in range(groups):
        sl = slice(g * width, (g + 1) * width)
        parts.append(_rms(x[:, sl], w[:, sl]))
    return jnp.concatenate(parts, axis=-1)


def _gelu(x):
    a = -2.0 * SQRT_2_OVER_PI * LOG2E
    return x / (1.0 + jnp.exp2(x * (a + (a * 0.044715) * (x * x))))


def _level_operands(q, k, b, half):
    rows, width = b.shape
    n = 2 * half
    if half >= SUBLANES:
        shape = (rows // n, 2, half, width)
        b4, q4, k4 = b.reshape(shape), q.reshape(shape), k.reshape(shape)
        first, second = b4[:, 0], b4[:, 1]
        mid = first[:, half - 1:half, :]
        zero = jnp.zeros_like(first)
        qe = jnp.stack([zero, q4[:, 1] * jnp.exp2(second - mid)], axis=1)
        ke = jnp.stack([k4[:, 0] * jnp.exp2(mid - first), zero], axis=1)
        return qe.reshape(rows, width), ke.reshape(rows, width)
    b3 = b.reshape(rows // SUBLANES, SUBLANES, width)
    pos = lax.broadcasted_iota(jnp.int32, b3.shape, 1) % n
    if half == SUBLANES // 2:
        mid = jnp.broadcast_to(b3[:, half - 1:half, :], b3.shape)
    else:
        assert half == 2
        prev1 = pltpu.roll(b3, 1, 1)
        nxt1 = pltpu.roll(b3, SUBLANES - 1, 1)
        prev2 = pltpu.roll(b3, 2, 1)
        mid = jnp.where(pos == 0, nxt1, jnp.where(pos == 1, b3, jnp.where(pos == 2, prev1, prev2)))
    sign = jnp.where(pos < half, -1.0, 1.0)
    e = jnp.exp2((b3 - mid) * sign).reshape(rows, width)
    return q * e, k * e


def _mixer_kernel(layer, x_ref, nmix_ref, win_ref, sw_ref, sbt_ref, nv_ref, nao_ref, lbs_ref,
                  nbo_ref, wout_ref, o_ref,
                  st_ref, z_ref, vn_ref, qs_ref, kst_ref, qe_ref, ke_ref, v_ref, eb_ref, ob_ref, y_ref):
    rows = x_ref.shape[0]

    @pl.when(pl.program_id(1) == 0)
    def _():
        st_ref[...] = jnp.zeros_like(st_ref)

    x = x_ref[...]
    h = _rms(x, nmix_ref[...])
    z_ref[...] = _dot(h.astype(BF16), win_ref[...])

    u = _gelu(z_ref[:, 0:D_A])
    vn_ref[...] = _group_rms(_gelu(z_ref[:, D_A:2 * D_A]), nv_ref[...], A_GROUPS).astype(BF16)
    tri = (lax.broadcasted_iota(jnp.int32, (A_CHUNK, A_CHUNK), 0)
           >= lax.broadcasted_iota(jnp.int32, (A_CHUNK, A_CHUNK), 1))
    ya_parts = []
    for g in range(A_GROUPS):
        w_g = jnp.where(tri, sw_ref[g], 0.0).astype(BF16)
        bias_g = sbt_ref[:, g:g + 1]
        sl = slice(g * HEAD, (g + 1) * HEAD)
        blocks = []
        for c in range(rows // A_CHUNK):
            rs = slice(c * A_CHUNK, (c + 1) * A_CHUNK)
            mixed = _dot(w_g, vn_ref[rs, sl]) + bias_g
            blocks.append(u[rs, sl] * mixed)
        y_g = jnp.concatenate(blocks, axis=0)
        ya_parts.append(_rms(y_g, nao_ref[:, sl]))
    y_ref[:, 0:D_A] = jnp.concatenate(ya_parts, axis=-1).astype(BF16)

    base = 2 * D_A
    zq = z_ref[:, base:base + D_B]
    zf = z_ref[:, base + D_B:base + 2 * D_B]
    v_ref[...] = z_ref[:, base + 2 * D_B:base + 3 * D_B].astype(BF16)

    soft2 = jnp.log2(1.0 + jnp.exp2(-jnp.abs(zf) * LOG2E))
    log2_sig = jnp.minimum(zf, 0.0) * LOG2E - soft2
    if layer == 0:
        g2 = log2_sig
    else:
        lbs = lbs_ref[...]
        sm = jnp.exp(lbs - jnp.max(lbs, axis=0, keepdims=True))
        sm = sm / jnp.sum(sm, axis=0, keepdims=True)
        lb = jnp.sum(sm[1:layer + 1, :], axis=0, keepdims=True)
        t0 = jnp.log2(lb)
        t1 = jnp.log1p(-lb) * LOG2E + log2_sig
        g2 = jnp.maximum(t0, t1) + jnp.log2(1.0 + jnp.exp2(-jnp.abs(t0 - t1)))
    f = jnp.exp2(g2)
    kk = 1.0 - f

    g_hi = g2.astype(BF16)
    g_lo = (g2 - g_hi.astype(F32)).astype(BF16)
    ltri = (lax.broadcasted_iota(jnp.int32, (B_CHUNK, B_CHUNK), 0)
            >= lax.broadcasted_iota(jnp.int32, (B_CHUNK, B_CHUNK), 1)).astype(BF16)
    b_parts = []
    for c in range(rows // B_CHUNK):
        rs = slice(c * B_CHUNK, (c + 1) * B_CHUNK)
        b_parts.append(_dot(ltri, g_hi[rs]) + _dot(ltri, g_lo[rs]))
    b = jnp.concatenate(b_parts, axis=0)

    def put_kt(lvl, k_lvl):
        for hd in range(B_HEADS):
            kst_ref[lvl, hd] = k_lvl[:, hd * HEAD:(hd + 1) * HEAD].T.astype(BF16)

    for lvl in range(1, N_LEVELS):
        q_lvl, k_lvl = _level_operands(zq, kk, b, 1 << lvl)
        qs_ref[lvl - 1] = q_lvl.astype(BF16)
        put_kt(lvl - 1, k_lvl)
    qf = zq * f

    b3 = b.reshape(rows // B_CHUNK, B_CHUNK, D_B)
    b_last = b3[:, B_CHUNK - 1:B_CHUNK, :]
    qe_ref[...] = (zq * jnp.exp2(b)).astype(BF16)
    ke_ref[...] = (kk.reshape(b3.shape) * jnp.exp2(b_last - b3)).reshape(rows, D_B).astype(BF16)
    eb_ref[...] = jnp.exp2(b_last).reshape(rows // B_CHUNK, D_B)

    ri = lax.broadcasted_iota(jnp.int32, (B_CHUNK, B_CHUNK), 0)
    ci = lax.broadcasted_iota(jnp.int32, (B_CHUNK, B_CHUNK), 1)
    xor = ri ^ ci
    masks = [(ri > ci) & (xor >= (1 << lvl)) & (xor < (2 << lvl)) for lvl in range(N_LEVELS)]
    masks.append(ri == ci)

    for c in range(rows // B_CHUNK):
        rs = slice(c * B_CHUNK, (c + 1) * B_CHUNK)
        decay = eb_ref[c:c + 1, :]
        for hd in range(B_HEADS):
            sl = slice(hd * HEAD, (hd + 1) * HEAD)
            kc = kk[rs, sl]
            k_prev = pltpu.roll(kc.reshape(B_CHUNK // SUBLANES, SUBLANES, HEAD), 1, 1)
            pair = jnp.sum(qf[rs, sl] * k_prev.reshape(B_CHUNK, HEAD), axis=-1, keepdims=True)
            diag = jnp.sum(zq[rs, sl] * kc, axis=-1, keepdims=True)
            scores = jnp.where(masks[N_LEVELS], diag, jnp.where(masks[0], pair, 0.0))
            for lvl in range(1, N_LEVELS):
                k_t = kst_ref[lvl - 1, hd, :, rs]
                scores = jnp.where(masks[lvl], _dot(qs_ref[lvl - 1, rs, sl], k_t), scores)
            vh = v_ref[rs, sl]
            st = st_ref[hd]
            o = _dot(scores.astype(BF16), vh) + _dot(qe_ref[rs, sl], st.T.astype(BF16))
            ob_ref[rs, sl] = o
            st_ref[hd] = st * decay[:, sl] + _dot_tn(vh, ke_ref[rs, sl])

    zg = z_ref[:, base + 3 * D_B:base + 4 * D_B]
    gate = zg / (1.0 + jnp.exp2(zg * -LOG2E))
    y_ref[:, D_A:] = (_group_rms(ob_ref[...], nbo_ref[...], B_HEADS) * gate).astype(BF16)

    o_ref[...] = x + _dot(y_ref[...], wout_ref[...])


def _mlp_kernel(final, x_ref, nw_ref, wup_ref, wdn_ref, nf_ref, o_ref):
    x = x_ref[...]
    h = _rms(x, nw_ref[...]).astype(BF16)
    acc = x
    for j in range(D_FF // FF_BLOCK):
        cs = slice(j * FF_BLOCK, (j + 1) * FF_BLOCK)
        a = jnp.maximum(_dot(h, wup_ref[:, cs]), 0.0)
        acc = acc + _dot((a * a).astype(BF16), wdn_ref[cs, :])
    if final:
        acc = _rms(acc, nf_ref[...])
    o_ref[...] = acc


def _full(shape):
    return pl.BlockSpec(shape, lambda *_: (0,) * len(shape))


def _mixer(layer, x, nmix, win, sw, sbt, nv, nao, lbs, nbo, wout):
    bsz, seq, _ = x.shape
    rows = MIX_ROWS
    n_chunks = rows // B_CHUNK
    xspec = pl.BlockSpec((None, rows, D_MODEL), lambda bi, si: (bi, si, 0))
    return pl.pallas_call(
        functools.partial(_mixer_kernel, layer),
        grid=(bsz, seq // rows),
        in_specs=[xspec, _full(nmix.shape), _full(win.shape), _full(sw.shape), _full(sbt.shape),
                  _full(nv.shape), _full(nao.shape), _full(lbs.shape), _full(nbo.shape),
                  _full(wout.shape)],
        out_specs=xspec,
        out_shape=jax.ShapeDtypeStruct(x.shape, F32),
        scratch_shapes=[
            pltpu.VMEM((B_HEADS, HEAD, HEAD), F32),
            pltpu.VMEM((rows, D_IN), F32),
            pltpu.VMEM((rows, D_A), BF16),
            pltpu.VMEM((N_LEVELS - 1, rows, D_B), BF16),
            pltpu.VMEM((N_LEVELS - 1, B_HEADS, HEAD, rows), BF16),
            pltpu.VMEM((rows, D_B), BF16),
            pltpu.VMEM((rows, D_B), BF16),
            pltpu.VMEM((rows, D_B), BF16),
            pltpu.VMEM((n_chunks, D_B), F32),
            pltpu.VMEM((rows, D_B), F32),
            pltpu.VMEM((rows, D_MODEL), BF16),
        ],
        compiler_params=pltpu.CompilerParams(
            dimension_semantics=("arbitrary", "arbitrary"), vmem_limit_bytes=VMEM_LIMIT_BYTES),
        name=f"mixer{layer}",
    )(x, nmix, win, sw, sbt, nv, nao, lbs, nbo, wout)


def _mlp(layer, final, x2, nw, wup, wdn, nf):
    n_tok = x2.shape[0]
    xspec = pl.BlockSpec((MLP_ROWS, D_MODEL), lambda i: (i, 0))
    return pl.pallas_call(
        functools.partial(_mlp_kernel, final),
        grid=(n_tok // MLP_ROWS,),
        in_specs=[xspec, _full(nw.shape), _full(wup.shape), _full(wdn.shape), _full(nf.shape)],
        out_specs=xspec,
        out_shape=jax.ShapeDtypeStruct(x2.shape, F32),
        compiler_params=pltpu.CompilerParams(
            dimension_semantics=("arbitrary",), vmem_limit_bytes=VMEM_LIMIT_BYTES),
        name=f"mlp{layer}",
    )(x2, nw, wup, wdn, nf)


def kernel(x, norm_mix, w_in, spatial_w, spatial_b, norm_v, norm_a_out, lower_bounds, norm_b_out,
           w_out, norm_mlp, w_up, w_down, norm_final):
    bsz, seq, d = x.shape
    depth = w_in.shape[0]
    assert d == D_MODEL and seq % MIX_ROWS == 0 and (bsz * seq) % MLP_ROWS == 0
    row = lambda a: a.reshape(1, -1).astype(F32)
    for l in range(depth):
        x = _mixer(l, x, row(norm_mix[l]), w_in[l].astype(BF16), spatial_w[l], spatial_b[l].T,
                   row(norm_v[l]), row(norm_a_out[l]), lower_bounds.astype(F32), row(norm_b_out[l]),
                   w_out[l].astype(BF16))
        x2 = _mlp(l, l == depth - 1, x.reshape(bsz * seq, d), row(norm_mlp[l]),
                  w_up[l].astype(BF16), w_down[l].astype(BF16), row(norm_final))
        x = x2.reshape(bsz, seq, d)
    return x
```

```python
import functools

import jax
import jax.numpy as jnp
from jax import lax
from jax.experimental import pallas as pl
from jax.experimental.pallas import tpu as pltpu

D_MODEL = 1024
D_A = 512
D_B = 512
A_GROUPS = 4
A_CHUNK = 128
B_HEADS = 4
HEAD = 128
B_CHUNK = 128
D_IN = 2 * D_A + 4 * D_B
D_FF = 4 * D_MODEL
EPS = 1e-6
LOG2E = 1.4426950408889634
SQRT_2_OVER_PI = 0.7978845608028654

SUBLANES = 8
N_LEVELS = 7
VMEM_LIMIT_BYTES = 60 * 1024 * 1024

MIX_ROWS = 1024
MLP_ROWS = 1024
FF_BLOCK = 1024

BF16 = jnp.bfloat16
F32 = jnp.float32


def _dot(a, b):
    return jnp.dot(a, b, preferred_element_type=F32)


def _dot_tn(a, b):
    return lax.dot_general(a, b, (((0,), (0,)), ((), ())), preferred_element_type=F32)


def _rms(x, w):
    return x * lax.rsqrt(jnp.mean(x * x, axis=-1, keepdims=True) + EPS) * w


def _group_rms(x, w, groups):
    width = x.shape[-1] // groups
    parts = []
    for g in range(groups):
        sl = slice(g * width, (g + 1) * width)
        parts.append(_rms(x[:, sl], w[:, sl]))
    return jnp.concatenate(parts, axis=-1)


def _gelu(x):
    a = -2.0 * SQRT_2_OVER_PI * LOG2E
    return x / (1.0 + jnp.exp2(x * (a + (a * 0.044715) * (x * x))))


def _level_operands(q, k, b, half):
    rows, width = b.shape
    n = 2 * half
    if half >= SUBLANES:
        shape = (rows // n, 2, half, width)
        b4, q4, k4 = b.reshape(shape), q.reshape(shape), k.reshape(shape)
        first, second = b4[:, 0], b4[:, 1]
        mid = first[:, half - 1:half, :]
        zero = jnp.zeros_like(first)
        qe = jnp.stack([zero, q4[:, 1] * jnp.exp2(second - mid)], axis=1)
        ke = jnp.stack([k4[:, 0] * jnp.exp2(mid - first), zero], axis=1)
        return qe.reshape(rows, width), ke.reshape(rows, width)
    b3 = b.reshape(rows // SUBLANES, SUBLANES, width)
    pos = lax.broadcasted_iota(jnp.int32, b3.shape, 1) % n
    if half == SUBLANES // 2:
        mid = jnp.broadcast_to(b3[:, half - 1:half, :], b3.shape)
    else:
        assert half == 2
        prev1 = pltpu.roll(b3, 1, 1)
        nxt1 = pltpu.roll(b3, SUBLANES - 1, 1)
        prev2 = pltpu.roll(b3, 2, 1)
        mid = jnp.where(pos == 0, nxt1, jnp.where(pos == 1, b3, jnp.where(pos == 2, prev1, prev2)))
    sign = jnp.where(pos < half, -1.0, 1.0)
    e = jnp.exp2((b3 - mid) * sign).reshape(rows, width)
    return q * e, k * e


def _mixer_kernel(layer, x_ref, nmix_ref, win_ref, sw_ref, sbt_ref, nv_ref, nao_ref, lbs_ref,
                  nbo_ref, wout_ref, o_ref,
                  st_ref, z_ref, vn_ref, qs_ref, kst_ref, qe_ref, ke_ref, v_ref, eb_ref, ob_ref, y_ref):
    rows = x_ref.shape[0]

    @pl.when(pl.program_id(1) == 0)
    def _():
        st_ref[...] = jnp.zeros_like(st_ref)

    x = x_ref[...]
    h = _rms(x, nmix_ref[...])
    z_ref[...] = _dot(h.astype(BF16), win_ref[...])

    u = _gelu(z_ref[:, 0:D_A])
    vn_ref[...] = _group_rms(_gelu(z_ref[:, D_A:2 * D_A]), nv_ref[...], A_GROUPS).astype(BF16)
    tri = (lax.broadcasted_iota(jnp.int32, (A_CHUNK, A_CHUNK), 0)
           >= lax.broadcasted_iota(jnp.int32, (A_CHUNK, A_CHUNK), 1))
    ya_parts = []
    for g in range(A_GROUPS):
        w_g = jnp.where(tri, sw_ref[g], 0.0).astype(BF16)
        bias_g = sbt_ref[:, g:g + 1]
        sl = slice(g * HEAD, (g + 1) * HEAD)
        blocks = []
        for c in range(rows // A_CHUNK):
            rs = slice(c * A_CHUNK, (c + 1) * A_CHUNK)
            mixed = _dot(w_g, vn_ref[rs, sl]) + bias_g
            blocks.append(u[rs, sl] * mixed)
        y_g = jnp.concatenate(blocks, axis=0)
        ya_parts.append(_rms(y_g, nao_ref[:, sl]))
    y_ref[:, 0:D_A] = jnp.concatenate(ya_parts, axis=-1).astype(BF16)

    base = 2 * D_A
    zq = z_ref[:, base:base + D_B]
    zf = z_ref[:, base + D_B:base + 2 * D_B]
    v_ref[...] = z_ref[:, base + 2 * D_B:base + 3 * D_B].astype(BF16)

    soft2 = jnp.log2(1.0 + jnp.exp2(-jnp.abs(zf) * LOG2E))
    log2_sig = jnp.minimum(zf, 0.0) * LOG2E - soft2
    if layer == 0:
        g2 = log2_sig
    else:
        lbs = lbs_ref[...]
        sm = jnp.exp(lbs - jnp.max(lbs, axis=0, keepdims=True))
        sm = sm / jnp.sum(sm, axis=0, keepdims=True)
        lb = jnp.sum(sm[1:layer + 1, :], axis=0, keepdims=True)
        t0 = jnp.log2(lb)
        t1 = jnp.log1p(-lb) * LOG2E + log2_sig
        g2 = jnp.maximum(t0, t1) + jnp.log2(1.0 + jnp.exp2(-jnp.abs(t0 - t1)))
    f = jnp.exp2(g2)
    kk = 1.0 - f

    g_hi = g2.astype(BF16)
    g_lo = (g2 - g_hi.astype(F32)).astype(BF16)
    ltri = (lax.broadcasted_iota(jnp.int32, (B_CHUNK, B_CHUNK), 0)
            >= lax.broadcasted_iota(jnp.int32, (B_CHUNK, B_CHUNK), 1)).astype(BF16)
    b_parts = []
    for c in range(rows // B_CHUNK):
        rs = slice(c * B_CHUNK, (c + 1) * B_CHUNK)
        b_parts.append(_dot(ltri, g_hi[rs]) + _dot(ltri, g_lo[rs]))
    b = jnp.concatenate(b_parts, axis=0)

    def put_kt(lvl, k_lvl):
        for hd in range(B_HEADS):
            kst_ref[lvl, hd] = k_lvl[:, hd * HEAD:(hd + 1) * HEAD].T.astype(BF16)

    for lvl in range(1, N_LEVELS):
        q_lvl, k_lvl = _level_operands(zq, kk, b, 1 << lvl)
        qs_ref[lvl - 1] = q_lvl.astype(BF16)
        put_kt(lvl - 1, k_lvl)
    qf = zq * f

    b3 = b.reshape(rows // B_CHUNK, B_CHUNK, D_B)
    b_last = b3[:, B_CHUNK - 1:B_CHUNK, :]
    qe_ref[...] = (zq * jnp.exp2(b)).astype(BF16)
    ke_ref[...] = (kk.reshape(b3.shape) * jnp.exp2(b_last - b3)).reshape(rows, D_B).astype(BF16)
    eb_ref[...] = jnp.exp2(b_last).reshape(rows // B_CHUNK, D_B)

    ri = lax.broadcasted_iota(jnp.int32, (B_CHUNK, B_CHUNK), 0)
    ci = lax.broadcasted_iota(jnp.int32, (B_CHUNK, B_CHUNK), 1)
    xor = ri ^ ci
    masks = [(ri > ci) & (xor >= (1 << lvl)) & (xor < (2 << lvl)) for lvl in range(N_LEVELS)]
    masks.append(ri == ci)

    for c in range(rows // B_CHUNK):
        rs = slice(c * B_CHUNK, (c + 1) * B_CHUNK)
        decay = eb_ref[c:c + 1, :]
        for hd in range(B_HEADS):
            sl = slice(hd * HEAD, (hd + 1) * HEAD)
            kc = kk[rs, sl]
            k_prev = pltpu.roll(kc.reshape(B_CHUNK // SUBLANES, SUBLANES, HEAD), 1, 1)
            pair = jnp.sum(qf[rs, sl] * k_prev.reshape(B_CHUNK, HEAD), axis=-1, keepdims=True)
            diag = jnp.sum(zq[rs, sl] * kc, axis=-1, keepdims=True)
            scores = jnp.where(masks[N_LEVELS], diag, jnp.where(masks[0], pair, 0.0))
            for lvl in range(1, N_LEVELS):
                k_t = kst_ref[lvl - 1, hd, :, rs]
                scores = jnp.where(masks[lvl], _dot(qs_ref[lvl - 1, rs, sl], k_t), scores)
            vh = v_ref[rs, sl]
            st = st_ref[hd]
            o = _dot(scores.astype(BF16), vh) + _dot(qe_ref[rs, sl], st.T.astype(BF16))
            ob_ref[rs, sl] = o
            st_ref[hd] = st * decay[:, sl] + _dot_tn(vh, ke_ref[rs, sl])

    zg = z_ref[:, base + 3 * D_B:base + 4 * D_B]
    gate = zg / (1.0 + jnp.exp2(zg * -LOG2E))
    y_ref[:, D_A:] = (_group_rms(ob_ref[...], nbo_ref[...], B_HEADS) * gate).astype(BF16)

    o_ref[...] = x + _dot(y_ref[...], wout_ref[...])


def _mlp_kernel(final, x_ref, nw_ref, wup_ref, wdn_ref, nf_ref, o_ref):
    x = x_ref[...]
    h = _rms(x, nw_ref[...]).astype(BF16)
    acc = x
    for j in range(D_FF // FF_BLOCK):
        cs = slice(j * FF_BLOCK, (j + 1) * FF_BLOCK)
        a = jnp.maximum(_dot(h, wup_ref[:, cs].astype(BF16)), 0.0)
        acc = acc + _dot((a * a).astype(BF16), wdn_ref[cs, :].astype(BF16))
    if final:
        acc = _rms(acc, nf_ref[...])
    o_ref[...] = acc


def _full(shape):
    return pl.BlockSpec(shape, lambda *_: (0,) * len(shape))


def _mixer(layer, x, nmix, win, sw, sbt, nv, nao, lbs, nbo, wout):
    bsz, seq, _ = x.shape
    rows = MIX_ROWS
    n_chunks = rows // B_CHUNK
    xspec = pl.BlockSpec((None, rows, D_MODEL), lambda bi, si: (bi, si, 0))
    return pl.pallas_call(
        functools.partial(_mixer_kernel, layer),
        grid=(bsz, seq // rows),
        in_specs=[xspec, _full(nmix.shape), _full(win.shape), _full(sw.shape), _full(sbt.shape),
                  _full(nv.shape), _full(nao.shape), _full(lbs.shape), _full(nbo.shape),
                  _full(wout.shape)],
        out_specs=xspec,
        out_shape=jax.ShapeDtypeStruct(x.shape, F32),
        scratch_shapes=[
            pltpu.VMEM((B_HEADS, HEAD, HEAD), F32),
            pltpu.VMEM((rows, D_IN), F32),
            pltpu.VMEM((rows, D_A), BF16),
            pltpu.VMEM((N_LEVELS - 1, rows, D_B), BF16),
            pltpu.VMEM((N_LEVELS - 1, B_HEADS, HEAD, rows), BF16),
            pltpu.VMEM((rows, D_B), BF16),
            pltpu.VMEM((rows, D_B), BF16),
            pltpu.VMEM((rows, D_B), BF16),
            pltpu.VMEM((n_chunks, D_B), F32),
            pltpu.VMEM((rows, D_B), F32),
            pltpu.VMEM((rows, D_MODEL), BF16),
        ],
        compiler_params=pltpu.CompilerParams(
            dimension_semantics=("arbitrary", "arbitrary"), vmem_limit_bytes=VMEM_LIMIT_BYTES),
        name=f"mixer{layer}",
    )(x, nmix, win, sw, sbt, nv, nao, lbs, nbo, wout)


def _mlp(layer, final, x2, nw, w_up, w_down, nf):
    n_tok = x2.shape[0]
    xspec = pl.BlockSpec((MLP_ROWS, D_MODEL), lambda i: (i, 0))
    of_layer = lambda i: (layer, 0, 0)
    return pl.pallas_call(
        functools.partial(_mlp_kernel, final),
        grid=(n_tok // MLP_ROWS,),
        in_specs=[xspec, _full(nw.shape), pl.BlockSpec((None, D_MODEL, D_FF), of_layer),
                  pl.BlockSpec((None, D_FF, D_MODEL), of_layer), _full(nf.shape)],
        out_specs=xspec,
        out_shape=jax.ShapeDtypeStruct(x2.shape, F32),
        compiler_params=pltpu.CompilerParams(
            dimension_semantics=("arbitrary",), vmem_limit_bytes=VMEM_LIMIT_BYTES),
        name=f"mlp{layer}",
    )(x2, nw, w_up, w_down, nf)


def kernel(x, norm_mix, w_in, spatial_w, spatial_b, norm_v, norm_a_out, lower_bounds, norm_b_out,
           w_out, norm_mlp, w_up, w_down, norm_final):
    bsz, seq, d = x.shape
    depth = w_in.shape[0]
    assert d == D_MODEL and seq % MIX_ROWS == 0 and (bsz * seq) % MLP_ROWS == 0
    row = lambda a: a.reshape(1, -1).astype(F32)
    for l in range(depth):
        x = _mixer(l, x, row(norm_mix[l]), w_in[l].astype(BF16), spatial_w[l], spatial_b[l].T,
                   row(norm_v[l]), row(norm_a_out[l]), lower_bounds.astype(F32), row(norm_b_out[l]),
                   w_out[l].astype(BF16))
        x2 = _mlp(l, l == depth - 1, x.reshape(bsz * seq, d), row(norm_mlp[l]),
                  w_up, w_down, row(norm_final))
        x = x2.reshape(bsz, seq, d)
    return x
```

```python
import functools

import jax
import jax.numpy as jnp
from jax import lax
from jax.experimental import pallas as pl
from jax.experimental.pallas import tpu as pltpu

D_MODEL = 1024
D_A = 512
D_B = 512
A_GROUPS = 4
A_CHUNK = 128
B_HEADS = 4
HEAD = 128
B_CHUNK = 128
D_IN = 2 * D_A + 4 * D_B
D_FF = 4 * D_MODEL
EPS = 1e-6
LOG2E = 1.4426950408889634
SQRT_2_OVER_PI = 0.7978845608028654

SUBLANES = 8
N_LEVELS = 7
VMEM_LIMIT_BYTES = 60 * 1024 * 1024

MIX_ROWS = 1024
MLP_ROWS = 1024
FF_BLOCK = 1024

BF16 = jnp.bfloat16
F32 = jnp.float32


def _dot(a, b):
    return jnp.dot(a, b, preferred_element_type=F32)


def _dot_tn(a, b):
    return lax.dot_general(a, b, (((0,), (0,)), ((), ())), preferred_element_type=F32)


def _rms(x, w):
    return x * lax.rsqrt(jnp.mean(x * x, axis=-1, keepdims=True) + EPS) * w


def _group_rms(x, w, groups):
    width = x.shape[-1] // groups
    parts = []
    for g in range(groups):
        sl = slice(g * width, (g + 1) * width)
        parts.append(_rms(x[:, sl], w[:, sl]))
    return jnp.concatenate(parts, axis=-1)


def _gelu(x):
    a = -2.0 * SQRT_2_OVER_PI * LOG2E
    return x / (1.0 + jnp.exp2(x * (a + (a * 0.044715) * (x * x))))


def _level_operands(q, k, b, half):
    rows, width = b.shape
    n = 2 * half
    if half >= SUBLANES:
        shape = (rows // n, 2, half, width)
        b4, q4, k4 = b.reshape(shape), q.reshape(shape), k.reshape(shape)
        first, second = b4[:, 0], b4[:, 1]
        mid = first[:, half - 1:half, :]
        zero = jnp.zeros_like(first)
        qe = jnp.stack([zero, q4[:, 1] * jnp.exp2(second - mid)], axis=1)
        ke = jnp.stack([k4[:, 0] * jnp.exp2(mid - first), zero], axis=1)
        return qe.reshape(rows, width), ke.reshape(rows, width)
    b3 = b.reshape(rows // SUBLANES, SUBLANES, width)
    pos = lax.broadcasted_iota(jnp.int32, b3.shape, 1) % n
    if half == SUBLANES // 2:
        mid = jnp.broadcast_to(b3[:, half - 1:half, :], b3.shape)
    else:
        assert half == 2
        prev1 = pltpu.roll(b3, 1, 1)
        nxt1 = pltpu.roll(b3, SUBLANES - 1, 1)
        prev2 = pltpu.roll(b3, 2, 1)
        mid = jnp.where(pos == 0, nxt1, jnp.where(pos == 1, b3, jnp.where(pos == 2, prev1, prev2)))
    sign = jnp.where(pos < half, -1.0, 1.0)
    e = jnp.exp2((b3 - mid) * sign).reshape(rows, width)
    return q * e, k * e


def _mixer_kernel(layer, x_ref, nmix_ref, win_ref, sw_ref, sbt_ref, nv_ref, nao_ref, lbs_ref,
                  nbo_ref, wout_ref, o_ref,
                  st_ref, z_ref, vn_ref, qs_ref, kst_ref, qe_ref, ke_ref, v_ref, eb_ref, y_ref):
    rows = x_ref.shape[0]

    @pl.when(pl.program_id(1) == 0)
    def _():
        st_ref[...] = jnp.zeros_like(st_ref)

    x = x_ref[...]
    h = _rms(x, nmix_ref[...])
    z_ref[...] = _dot(h.astype(BF16), win_ref[...])

    u = _gelu(z_ref[:, 0:D_A])
    vn_ref[...] = _group_rms(_gelu(z_ref[:, D_A:2 * D_A]), nv_ref[...], A_GROUPS).astype(BF16)
    tri = (lax.broadcasted_iota(jnp.int32, (A_CHUNK, A_CHUNK), 0)
           >= lax.broadcasted_iota(jnp.int32, (A_CHUNK, A_CHUNK), 1))
    ya_parts = []
    for g in range(A_GROUPS):
        w_g = jnp.where(tri, sw_ref[g], 0.0).astype(BF16)
        bias_g = sbt_ref[:, g:g + 1]
        sl = slice(g * HEAD, (g + 1) * HEAD)
        blocks = []
        for c in range(rows // A_CHUNK):
            rs = slice(c * A_CHUNK, (c + 1) * A_CHUNK)
            mixed = _dot(w_g, vn_ref[rs, sl]) + bias_g
            blocks.append(u[rs, sl] * mixed)
        y_g = jnp.concatenate(blocks, axis=0)
        ya_parts.append(_rms(y_g, nao_ref[:, sl]))
    y_ref[:, 0:D_A] = jnp.concatenate(ya_parts, axis=-1).astype(BF16)

    base = 2 * D_A
    zq = z_ref[:, base:base + D_B]
    zf = z_ref[:, base + D_B:base + 2 * D_B]
    v_ref[...] = z_ref[:, base + 2 * D_B:base + 3 * D_B].astype(BF16)

    zc = zf * LOG2E
    log2_sig = jnp.minimum(zc, 0.0) - jnp.log2(1.0 + jnp.exp2(-jnp.abs(zc)))
    if layer == 0:
        g2 = log2_sig
    else:
        lbs = lbs_ref[...]
        sm = jnp.exp(lbs - jnp.max(lbs, axis=0, keepdims=True))
        sm = sm / jnp.sum(sm, axis=0, keepdims=True)
        lb = jnp.sum(sm[1:layer + 1, :], axis=0, keepdims=True)
        t0 = jnp.log2(lb)
        t1 = jnp.log1p(-lb) * LOG2E + log2_sig
        g2 = jnp.maximum(t0, t1) + jnp.log2(1.0 + jnp.exp2(-jnp.abs(t0 - t1)))
    f = jnp.exp2(g2)
    kk = 1.0 - f

    g_hi = g2.astype(BF16)
    g_lo = (g2 - g_hi.astype(F32)).astype(BF16)
    ltri = (lax.broadcasted_iota(jnp.int32, (B_CHUNK, B_CHUNK), 0)
            >= lax.broadcasted_iota(jnp.int32, (B_CHUNK, B_CHUNK), 1)).astype(BF16)
    b_parts = []
    for c in range(rows // B_CHUNK):
        rs = slice(c * B_CHUNK, (c + 1) * B_CHUNK)
        b_parts.append(_dot(ltri, g_hi[rs]) + _dot(ltri, g_lo[rs]))
    b = jnp.concatenate(b_parts, axis=0)

    def put_kt(lvl, k_lvl):
        for hd in range(B_HEADS):
            kst_ref[lvl, hd] = k_lvl[:, hd * HEAD:(hd + 1) * HEAD].T.astype(BF16)

    for lvl in range(1, N_LEVELS):
        q_lvl, k_lvl = _level_operands(zq, kk, b, 1 << lvl)
        qs_ref[lvl - 1] = q_lvl.astype(BF16)
        put_kt(lvl - 1, k_lvl)
    qf = zq * f

    b3 = b.reshape(rows // B_CHUNK, B_CHUNK, D_B)
    b_last = b3[:, B_CHUNK - 1:B_CHUNK, :]
    qe_ref[...] = (zq * jnp.exp2(b)).astype(BF16)
    ke_ref[...] = (kk.reshape(b3.shape) * jnp.exp2(b_last - b3)).reshape(rows, D_B).astype(BF16)
    eb_ref[...] = jnp.exp2(b_last).reshape(rows // B_CHUNK, D_B)

    zg = z_ref[:, base + 3 * D_B:base + 4 * D_B]
    gate = zg / (1.0 + jnp.exp2(zg * -LOG2E))

    ri = lax.broadcasted_iota(jnp.int32, (B_CHUNK, B_CHUNK), 0)
    ci = lax.broadcasted_iota(jnp.int32, (B_CHUNK, B_CHUNK), 1)
    xor = ri ^ ci
    masks = [(ri > ci) & (xor >= (1 << lvl)) & (xor < (2 << lvl)) for lvl in range(N_LEVELS)]
    masks.append(ri == ci)

    for c in range(rows // B_CHUNK):
        rs = slice(c * B_CHUNK, (c + 1) * B_CHUNK)
        decay = eb_ref[c:c + 1, :]
        for hd in range(B_HEADS):
            sl = slice(hd * HEAD, (hd + 1) * HEAD)
            kc = kk[rs, sl]
            k_prev = pltpu.roll(kc.reshape(B_CHUNK // SUBLANES, SUBLANES, HEAD), 1, 1)
            pair = jnp.sum(qf[rs, sl] * k_prev.reshape(B_CHUNK, HEAD), axis=-1, keepdims=True)
            diag = jnp.sum(zq[rs, sl] * kc, axis=-1, keepdims=True)
            groups = [slice(r, r + SUBLANES) for r in range(0, B_CHUNK, SUBLANES)]
            score_rows = [jnp.where(masks[N_LEVELS][g], diag[g], jnp.where(masks[0][g], pair[g], 0.0))
                          for g in groups]
            for lvl in range(1, N_LEVELS):
                half = 1 << lvl
                p = _dot(qs_ref[lvl - 1, rs, sl], kst_ref[lvl - 1, hd, :, rs])
                for i, g in enumerate(groups):
                    if half < SUBLANES or g.start & half:
                        score_rows[i] = jnp.where(masks[lvl][g], p[g], score_rows[i])
            scores = jnp.concatenate(score_rows, axis=0)
            vh = v_ref[rs, sl]
            st = st_ref[hd]
            o = _dot(scores.astype(BF16), vh) + _dot(qe_ref[rs, sl], st.T.astype(BF16))
            y_ref[rs, D_A + hd * HEAD:D_A + (hd + 1) * HEAD] = (
                _rms(o, nbo_ref[:, sl]) * gate[rs, sl]).astype(BF16)
            st_ref[hd] = st * decay[:, sl] + _dot_tn(vh, ke_ref[rs, sl])

    o_ref[...] = x + _dot(y_ref[...], wout_ref[...])


def _mlp_kernel(final, x_ref, nw_ref, wup_ref, wdn_ref, nf_ref, o_ref):
    x = x_ref[...]
    h = _rms(x, nw_ref[...]).astype(BF16)
    acc = x
    for j in range(D_FF // FF_BLOCK):
        cs = slice(j * FF_BLOCK, (j + 1) * FF_BLOCK)
        a = jnp.maximum(_dot(h, wup_ref[:, cs].astype(BF16)), 0.0)
        acc = acc + _dot((a * a).astype(BF16), wdn_ref[cs, :].astype(BF16))
    if final:
        acc = _rms(acc, nf_ref[...])
    o_ref[...] = acc


def _full(shape):
    return pl.BlockSpec(shape, lambda *_: (0,) * len(shape))


def _mixer(layer, x, nmix, win, sw, sbt, nv, nao, lbs, nbo, wout):
    bsz, seq, _ = x.shape
    rows = MIX_ROWS
    n_chunks = rows // B_CHUNK
    xspec = pl.BlockSpec((None, rows, D_MODEL), lambda bi, si: (bi, si, 0))
    return pl.pallas_call(
        functools.partial(_mixer_kernel, layer),
        grid=(bsz, seq // rows),
        in_specs=[xspec, _full(nmix.shape), _full(win.shape), _full(sw.shape), _full(sbt.shape),
                  _full(nv.shape), _full(nao.shape), _full(lbs.shape), _full(nbo.shape),
                  _full(wout.shape)],
        out_specs=xspec,
        out_shape=jax.ShapeDtypeStruct(x.shape, F32),
        scratch_shapes=[
            pltpu.VMEM((B_HEADS, HEAD, HEAD), F32),
            pltpu.VMEM((rows, D_IN), F32),
            pltpu.VMEM((rows, D_A), BF16),
            pltpu.VMEM((N_LEVELS - 1, rows, D_B), BF16),
            pltpu.VMEM((N_LEVELS - 1, B_HEADS, HEAD, rows), BF16),
            pltpu.VMEM((rows, D_B), BF16),
            pltpu.VMEM((rows, D_B), BF16),
            pltpu.VMEM((rows, D_B), BF16),
            pltpu.VMEM((n_chunks, D_B), F32),
            pltpu.VMEM((rows, D_MODEL), BF16),
        ],
        compiler_params=pltpu.CompilerParams(
            dimension_semantics=("arbitrary", "arbitrary"), vmem_limit_bytes=VMEM_LIMIT_BYTES),
        name=f"mixer{layer}",
    )(x, nmix, win, sw, sbt, nv, nao, lbs, nbo, wout)


def _mlp(layer, final, x2, nw, w_up, w_down, nf):
    n_tok = x2.shape[0]
    xspec = pl.BlockSpec((MLP_ROWS, D_MODEL), lambda i: (i, 0))
    of_layer = lambda i: (layer, 0, 0)
    return pl.pallas_call(
        functools.partial(_mlp_kernel, final),
        grid=(n_tok // MLP_ROWS,),
        in_specs=[xspec, _full(nw.shape), pl.BlockSpec((None, D_MODEL, D_FF), of_layer),
                  pl.BlockSpec((None, D_FF, D_MODEL), of_layer), _full(nf.shape)],
        out_specs=xspec,
        out_shape=jax.ShapeDtypeStruct(x2.shape, F32),
        compiler_params=pltpu.CompilerParams(
            dimension_semantics=("arbitrary",), vmem_limit_bytes=VMEM_LIMIT_BYTES),
        name=f"mlp{layer}",
    )(x2, nw, w_up, w_down, nf)


def kernel(x, norm_mix, w_in, spatial_w, spatial_b, norm_v, norm_a_out, lower_bounds, norm_b_out,
           w_out, norm_mlp, w_up, w_down, norm_final):
    bsz, seq, d = x.shape
    depth = w_in.shape[0]
    assert d == D_MODEL and seq % MIX_ROWS == 0 and (bsz * seq) % MLP_ROWS == 0
    row = lambda a: a.reshape(1, -1).astype(F32)
    for l in range(depth):
        x = _mixer(l, x, row(norm_mix[l]), w_in[l].astype(BF16), spatial_w[l], spatial_b[l].T,
                   row(norm_v[l]), row(norm_a_out[l]), lower_bounds.astype(F32), row(norm_b_out[l]),
                   w_out[l].astype(BF16))
        x2 = _mlp(l, l == depth - 1, x.reshape(bsz * seq, d), row(norm_mlp[l]),
                  w_up, w_down, row(norm_final))
        x = x2.reshape(bsz, seq, d)
    return x
```

```python
import functools

import jax
import jax.numpy as jnp
from jax import lax
from jax.experimental import pallas as pl
from jax.experimental.pallas import tpu as pltpu

D_MODEL = 1024
D_A = 512
D_B = 512
A_GROUPS = 4
A_CHUNK = 128
B_HEADS = 4
HEAD = 128
B_CHUNK = 128
D_IN = 2 * D_A + 4 * D_B
D_FF = 4 * D_MODEL
EPS = 1e-6
LOG2E = 1.4426950408889634
SQRT_2_OVER_PI = 0.7978845608028654

SUBLANES = 8
N_LEVELS = 7
FAST_BLOCK = 32
FAST_LEVELS = 5
SAFE_SPAN = 100.0
VMEM_LIMIT_BYTES = 60 * 1024 * 1024

MIX_ROWS = 512
MLP_ROWS = 1024
FF_BLOCK = 1024

BF16 = jnp.bfloat16
F32 = jnp.float32


def _dot(a, b):
    return jnp.dot(a, b, preferred_element_type=F32)


def _dot_tn(a, b):
    return lax.dot_general(a, b, (((0,), (0,)), ((), ())), preferred_element_type=F32)


def _rms(x, w):
    return x * lax.rsqrt(jnp.mean(x * x, axis=-1, keepdims=True) + EPS) * w


def _group_rms(x, w, groups):
    width = x.shape[-1] // groups
    parts = []
    for g in range(groups):
        sl = slice(g * width, (g + 1) * width)
        parts.append(_rms(x[:, sl], w[:, sl]))
    return jnp.concatenate(parts, axis=-1)


def _gelu(x):
    a = -2.0 * SQRT_2_OVER_PI * LOG2E
    return x / (1.0 + jnp.exp2(x * (a + (a * 0.044715) * (x * x))))


def _level_operands(q, k, b, half):
    rows, width = b.shape
    n = 2 * half
    if half >= SUBLANES:
        shape = (rows // n, 2, half, width)
        b4, q4, k4 = b.reshape(shape), q.reshape(shape), k.reshape(shape)
        first, second = b4[:, 0], b4[:, 1]
        mid = first[:, half - 1:half, :]
        zero = jnp.zeros_like(first)
        qe = jnp.stack([zero, q4[:, 1] * jnp.exp2(second - mid)], axis=1)
        ke = jnp.stack([k4[:, 0] * jnp.exp2(mid - first), zero], axis=1)
        return qe.reshape(rows, width), ke.reshape(rows, width)
    b3 = b.reshape(rows // SUBLANES, SUBLANES, width)
    pos = lax.broadcasted_iota(jnp.int32, b3.shape, 1) % n
    if half == SUBLANES // 2:
        mid = jnp.broadcast_to(b3[:, half - 1:half, :], b3.shape)
    else:
        assert half == 2
        prev1 = pltpu.roll(b3, 1, 1)
        nxt1 = pltpu.roll(b3, SUBLANES - 1, 1)
        prev2 = pltpu.roll(b3, 2, 1)
        mid = jnp.where(pos == 0, nxt1, jnp.where(pos == 1, b3, jnp.where(pos == 2, prev1, prev2)))
    sign = jnp.where(pos < half, -1.0, 1.0)
    e = jnp.exp2((b3 - mid) * sign).reshape(rows, width)
    return q * e, k * e


def _mixer_kernel(layer, x_ref, nmix_ref, win_ref, sw_ref, sbt_ref, nv_ref, nao_ref, lbs_ref,
                  nbo_ref, wout_ref, o_ref,
                  st_ref, z_ref, vn_ref, qs_ref, kst_ref, qfast_ref, kfast_ref, qe_ref, ke_ref,
                  v_ref, eb_ref, y_ref):
    rows = x_ref.shape[0]

    @pl.when(pl.program_id(1) == 0)
    def _():
        st_ref[...] = jnp.zeros_like(st_ref)

    x = x_ref[...]
    h = _rms(x, nmix_ref[...])
    z_ref[...] = _dot(h.astype(BF16), win_ref[...])

    u = _gelu(z_ref[:, 0:D_A])
    vn_ref[...] = _group_rms(_gelu(z_ref[:, D_A:2 * D_A]), nv_ref[...], A_GROUPS).astype(BF16)
    tri = (lax.broadcasted_iota(jnp.int32, (A_CHUNK, A_CHUNK), 0)
           >= lax.broadcasted_iota(jnp.int32, (A_CHUNK, A_CHUNK), 1))
    ya_parts = []
    for g in range(A_GROUPS):
        w_g = jnp.where(tri, sw_ref[g], 0.0).astype(BF16)
        bias_g = sbt_ref[:, g:g + 1]
        sl = slice(g * HEAD, (g + 1) * HEAD)
        blocks = []
        for c in range(rows // A_CHUNK):
            rs = slice(c * A_CHUNK, (c + 1) * A_CHUNK)
            mixed = _dot(w_g, vn_ref[rs, sl]) + bias_g
            blocks.append(u[rs, sl] * mixed)
        y_g = jnp.concatenate(blocks, axis=0)
        ya_parts.append(_rms(y_g, nao_ref[:, sl]))
    y_ref[:, 0:D_A] = jnp.concatenate(ya_parts, axis=-1).astype(BF16)

    base = 2 * D_A
    zq = z_ref[:, base:base + D_B]
    zf = z_ref[:, base + D_B:base + 2 * D_B]
    v_ref[...] = z_ref[:, base + 2 * D_B:base + 3 * D_B].astype(BF16)

    zc = zf * LOG2E
    log2_sig = jnp.minimum(zc, 0.0) - jnp.log2(1.0 + jnp.exp2(-jnp.abs(zc)))
    if layer == 0:
        g2 = log2_sig
    else:
        lbs = lbs_ref[...]
        sm = jnp.exp(lbs - jnp.max(lbs, axis=0, keepdims=True))
        sm = sm / jnp.sum(sm, axis=0, keepdims=True)
        lb = jnp.sum(sm[1:layer + 1, :], axis=0, keepdims=True)
        t0 = jnp.log2(lb)
        t1 = jnp.log1p(-lb) * LOG2E + log2_sig
        g2 = jnp.maximum(t0, t1) + jnp.log2(1.0 + jnp.exp2(-jnp.abs(t0 - t1)))
    f = jnp.exp2(g2)
    kk = 1.0 - f

    g_hi = g2.astype(BF16)
    g_lo = (g2 - g_hi.astype(F32)).astype(BF16)
    ltri = (lax.broadcasted_iota(jnp.int32, (B_CHUNK, B_CHUNK), 0)
            >= lax.broadcasted_iota(jnp.int32, (B_CHUNK, B_CHUNK), 1)).astype(BF16)
    b_parts = []
    for c in range(rows // B_CHUNK):
        rs = slice(c * B_CHUNK, (c + 1) * B_CHUNK)
        b_parts.append(_dot(ltri, g_hi[rs]) + _dot(ltri, g_lo[rs]))
    b = jnp.concatenate(b_parts, axis=0)

    def put_kt(ref, idx, k_part):
        for hd in range(B_HEADS):
            ref[idx, hd] = k_part[:, hd * HEAD:(hd + 1) * HEAD].T.astype(BF16)

    for lvl in range(FAST_LEVELS, N_LEVELS):
        q_lvl, k_lvl = _level_operands(zq, kk, b, 1 << lvl)
        qs_ref[lvl - 1] = q_lvl.astype(BF16)
        put_kt(kst_ref, lvl - 1, k_lvl)

    b3 = b.reshape(rows // B_CHUNK, B_CHUNK, D_B)
    b_last = b3[:, B_CHUNK - 1:B_CHUNK, :]
    qe_ref[...] = (zq * jnp.exp2(b)).astype(BF16)
    ke_ref[...] = (kk.reshape(b3.shape) * jnp.exp2(b_last - b3)).reshape(rows, D_B).astype(BF16)
    eb_ref[...] = jnp.exp2(b_last).reshape(rows // B_CHUNK, D_B)

    zg = z_ref[:, base + 3 * D_B:base + 4 * D_B]
    gate = zg / (1.0 + jnp.exp2(zg * -LOG2E))

    ri = lax.broadcasted_iota(jnp.int32, (B_CHUNK, B_CHUNK), 0)
    ci = lax.broadcasted_iota(jnp.int32, (B_CHUNK, B_CHUNK), 1)
    xor = ri ^ ci
    masks = [(ri > ci) & (xor >= (1 << lvl)) & (xor < (2 << lvl)) for lvl in range(N_LEVELS)]
    masks.append(ri == ci)

    def run_chunks(score_fn):
        for c in range(rows // B_CHUNK):
            rs = slice(c * B_CHUNK, (c + 1) * B_CHUNK)
            decay = eb_ref[c:c + 1, :]
            for hd in range(B_HEADS):
                sl = slice(hd * HEAD, (hd + 1) * HEAD)
                scores = score_fn(rs, sl, hd)
                vh = v_ref[rs, sl]
                st = st_ref[hd]
                o = _dot(scores.astype(BF16), vh) + _dot(qe_ref[rs, sl], st.T.astype(BF16))
                y_ref[rs, D_A + hd * HEAD:D_A + (hd + 1) * HEAD] = (
                    _rms(o, nbo_ref[:, sl]) * gate[rs, sl]).astype(BF16)
                st_ref[hd] = st * decay[:, sl] + _dot_tn(vh, ke_ref[rs, sl])

    def upper_levels(scores, rs, sl, hd):
        for lvl in range(FAST_LEVELS, N_LEVELS):
            p = _dot(qs_ref[lvl - 1, rs, sl], kst_ref[lvl - 1, hd, :, rs])
            scores = jnp.where(masks[lvl], p, scores)
        return scores

    nblk = rows // FAST_BLOCK
    b4 = b.reshape(nblk, FAST_BLOCK, D_B)
    r_blk = b4[:, 0:1, :] - g2.reshape(b4.shape)[:, 0:1, :]
    span = r_blk - b4[:, FAST_BLOCK - 1:FAST_BLOCK, :]
    fast = jnp.max(span) < SAFE_SPAN

    @pl.when(fast)
    def _():
        qfast_ref[...] = (zq.reshape(b4.shape) * jnp.exp2(b4 - r_blk)).reshape(rows, D_B).astype(BF16)
        put_kt(kfast_ref, 0, (kk.reshape(b4.shape) * jnp.exp2(r_blk - b4)).reshape(rows, D_B))
        same_block = (ri >= ci) & (xor < FAST_BLOCK)

        def scores_fast(rs, sl, hd):
            p = _dot(qfast_ref[rs, sl], kfast_ref[0, hd, :, rs])
            return upper_levels(jnp.where(same_block, p, 0.0), rs, sl, hd)

        run_chunks(scores_fast)

    @pl.when(jnp.logical_not(fast))
    def _():
        for lvl in range(1, FAST_LEVELS):
            q_lvl, k_lvl = _level_operands(zq, kk, b, 1 << lvl)
            qs_ref[lvl - 1] = q_lvl.astype(BF16)
            put_kt(kst_ref, lvl - 1, k_lvl)
        qf = zq * f

        def scores_any(rs, sl, hd):
            kc = kk[rs, sl]
            k_prev = pltpu.roll(kc.reshape(B_CHUNK // SUBLANES, SUBLANES, HEAD), 1, 1)
            pair = jnp.sum(qf[rs, sl] * k_prev.reshape(B_CHUNK, HEAD), axis=-1, keepdims=True)
            diag = jnp.sum(zq[rs, sl] * kc, axis=-1, keepdims=True)
            scores = jnp.where(masks[N_LEVELS], diag, jnp.where(masks[0], pair, 0.0))
            for lvl in range(1, FAST_LEVELS):
                p = _dot(qs_ref[lvl - 1, rs, sl], kst_ref[lvl - 1, hd, :, rs])
                scores = jnp.where(masks[lvl], p, scores)
            return upper_levels(scores, rs, sl, hd)

        run_chunks(scores_any)

    o_ref[...] = x + _dot(y_ref[...], wout_ref[...])


def _mlp_kernel(final, x_ref, nw_ref, wup_ref, wdn_ref, nf_ref, o_ref):
    x = x_ref[...]
    h = _rms(x, nw_ref[...]).astype(BF16)
    acc = x
    for j in range(D_FF // FF_BLOCK):
        cs = slice(j * FF_BLOCK, (j + 1) * FF_BLOCK)
        a = jnp.maximum(_dot(h, wup_ref[:, cs].astype(BF16)), 0.0)
        acc = acc + _dot((a * a).astype(BF16), wdn_ref[cs, :].astype(BF16))
    if final:
        acc = _rms(acc, nf_ref[...])
    o_ref[...] = acc


def _full(shape):
    return pl.BlockSpec(shape, lambda *_: (0,) * len(shape))


def _mixer(layer, x, nmix, win, sw, sbt, nv, nao, lbs, nbo, wout):
    bsz, seq, _ = x.shape
    rows = MIX_ROWS
    n_chunks = rows // B_CHUNK
    xspec = pl.BlockSpec((None, rows, D_MODEL), lambda bi, si: (bi, si, 0))
    return pl.pallas_call(
        functools.partial(_mixer_kernel, layer),
        grid=(bsz, seq // rows),
        in_specs=[xspec, _full(nmix.shape), _full(win.shape), _full(sw.shape), _full(sbt.shape),
                  _full(nv.shape), _full(nao.shape), _full(lbs.shape), _full(nbo.shape),
                  _full(wout.shape)],
        out_specs=xspec,
        out_shape=jax.ShapeDtypeStruct(x.shape, F32),
        scratch_shapes=[
            pltpu.VMEM((B_HEADS, HEAD, HEAD), F32),
            pltpu.VMEM((rows, D_IN), F32),
            pltpu.VMEM((rows, D_A), BF16),
            pltpu.VMEM((N_LEVELS - 1, rows, D_B), BF16),
            pltpu.VMEM((N_LEVELS - 1, B_HEADS, HEAD, rows), BF16),
            pltpu.VMEM((rows, D_B), BF16),
            pltpu.VMEM((1, B_HEADS, HEAD, rows), BF16),
            pltpu.VMEM((rows, D_B), BF16),
            pltpu.VMEM((rows, D_B), BF16),
            pltpu.VMEM((rows, D_B), BF16),
            pltpu.VMEM((n_chunks, D_B), F32),
            pltpu.VMEM((rows, D_MODEL), BF16),
        ],
        compiler_params=pltpu.CompilerParams(
            dimension_semantics=("arbitrary", "arbitrary"), vmem_limit_bytes=VMEM_LIMIT_BYTES),
        name=f"mixer{layer}",
    )(x, nmix, win, sw, sbt, nv, nao, lbs, nbo, wout)


def _mlp(layer, final, x2, nw, w_up, w_down, nf):
    n_tok = x2.shape[0]
    xspec = pl.BlockSpec((MLP_ROWS, D_MODEL), lambda i: (i, 0))
    of_layer = lambda i: (layer, 0, 0)
    return pl.pallas_call(
        functools.partial(_mlp_kernel, final),
        grid=(n_tok // MLP_ROWS,),
        in_specs=[xspec, _full(nw.shape), pl.BlockSpec((None, D_MODEL, D_FF), of_layer),
                  pl.BlockSpec((None, D_FF, D_MODEL), of_layer), _full(nf.shape)],
        out_specs=xspec,
        out_shape=jax.ShapeDtypeStruct(x2.shape, F32),
        compiler_params=pltpu.CompilerParams(
            dimension_semantics=("arbitrary",), vmem_limit_bytes=VMEM_LIMIT_BYTES),
        name=f"mlp{layer}",
    )(x2, nw, w_up, w_down, nf)


def kernel(x, norm_mix, w_in, spatial_w, spatial_b, norm_v, norm_a_out, lower_bounds, norm_b_out,
           w_out, norm_mlp, w_up, w_down, norm_final):
    bsz, seq, d = x.shape
    depth = w_in.shape[0]
    assert d == D_MODEL and seq % MIX_ROWS == 0 and (bsz * seq) % MLP_ROWS == 0
    row = lambda a: a.reshape(1, -1).astype(F32)
    for l in range(depth):
        x = _mixer(l, x, row(norm_mix[l]), w_in[l].astype(BF16), spatial_w[l], spatial_b[l].T,
                   row(norm_v[l]), row(norm_a_out[l]), lower_bounds.astype(F32), row(norm_b_out[l]),
                   w_out[l].astype(BF16))
        x2 = _mlp(l, l == depth - 1, x.reshape(bsz * seq, d), row(norm_mlp[l]),
                  w_up, w_down, row(norm_final))
        x = x2.reshape(bsz, seq, d)
    return x
```

```python
import functools

import jax
import jax.numpy as jnp
from jax import lax
from jax.experimental import pallas as pl
from jax.experimental.pallas import tpu as pltpu

D_MODEL = 1024
D_A = 512
D_B = 512
A_GROUPS = 4
A_CHUNK = 128
B_HEADS = 4
HEAD = 128
B_CHUNK = 128
D_IN = 2 * D_A + 4 * D_B
D_FF = 4 * D_MODEL
EPS = 1e-6
LOG2E = 1.4426950408889634
SQRT_2_OVER_PI = 0.7978845608028654

SUBLANES = 8
N_LEVELS = 7
VMEM_LIMIT_BYTES = 63 * 1024 * 1024

MIX_ROWS = 1024
IN_BLOCK = 512
MLP_ROWS = 1024
FF_BLOCK = 1024

BF16 = jnp.bfloat16
F32 = jnp.float32


def _dot(a, b):
    return jnp.dot(a, b, preferred_element_type=F32)


def _dot_tn(a, b):
    return lax.dot_general(a, b, (((0,), (0,)), ((), ())), preferred_element_type=F32)


def _rms(x, w):
    return x * lax.rsqrt(jnp.mean(x * x, axis=-1, keepdims=True) + EPS) * w


def _group_rms(x, w, groups):
    width = x.shape[-1] // groups
    parts = []
    for g in range(groups):
        sl = slice(g * width, (g + 1) * width)
        parts.append(_rms(x[:, sl], w[:, sl]))
    return jnp.concatenate(parts, axis=-1)


def _gelu(x):
    a = -2.0 * SQRT_2_OVER_PI * LOG2E
    return x / (1.0 + jnp.exp2(x * (a + (a * 0.044715) * (x * x))))


def _level_operands(q, k, b, half):
    rows, width = b.shape
    n = 2 * half
    if half >= SUBLANES:
        shape = (rows // n, 2, half, width)
        b4, q4, k4 = b.reshape(shape), q.reshape(shape), k.reshape(shape)
        first, second = b4[:, 0], b4[:, 1]
        mid = first[:, half - 1:half, :]
        zero = jnp.zeros_like(first)
        qe = jnp.stack([zero, q4[:, 1] * jnp.exp2(second - mid)], axis=1)
        ke = jnp.stack([k4[:, 0] * jnp.exp2(mid - first), zero], axis=1)
        return qe.reshape(rows, width), ke.reshape(rows, width)
    b3 = b.reshape(rows // SUBLANES, SUBLANES, width)
    pos = lax.broadcasted_iota(jnp.int32, b3.shape, 1) % n
    if half == SUBLANES // 2:
        mid = jnp.broadcast_to(b3[:, half - 1:half, :], b3.shape)
    else:
        assert half == 2
        prev1 = pltpu.roll(b3, 1, 1)
        nxt1 = pltpu.roll(b3, SUBLANES - 1, 1)
        prev2 = pltpu.roll(b3, 2, 1)
        mid = jnp.where(pos == 0, nxt1, jnp.where(pos == 1, b3, jnp.where(pos == 2, prev1, prev2)))
    sign = jnp.where(pos < half, -1.0, 1.0)
    e = jnp.exp2((b3 - mid) * sign).reshape(rows, width)
    return q * e, k * e


def _mixer_kernel(layer, x_ref, nmix_ref, win_ref, sw_ref, sbt_ref, nv_ref, nao_ref, lbs_ref,
                  nbo_ref, wout_ref, o_ref,
                  st_ref, z_ref, vn_ref, qs_ref, kst_ref, qe_ref, ke_ref, v_ref, eb_ref, ob_ref, y_ref):
    rows = x_ref.shape[0]

    @pl.when(pl.program_id(1) == 0)
    def _():
        st_ref[...] = jnp.zeros_like(st_ref)

    x = x_ref[...]
    h = _rms(x, nmix_ref[...])
    hb = h.astype(BF16)
    for j in range(D_IN // IN_BLOCK):
        cs = slice(j * IN_BLOCK, (j + 1) * IN_BLOCK)
        z_ref[:, cs] = _dot(hb, win_ref[:, cs].astype(BF16))

    u = _gelu(z_ref[:, 0:D_A])
    vn_ref[...] = _group_rms(_gelu(z_ref[:, D_A:2 * D_A]), nv_ref[...], A_GROUPS).astype(BF16)
    tri = (lax.broadcasted_iota(jnp.int32, (A_CHUNK, A_CHUNK), 0)
           >= lax.broadcasted_iota(jnp.int32, (A_CHUNK, A_CHUNK), 1))
    ya_parts = []
    for g in range(A_GROUPS):
        w_g = jnp.where(tri, sw_ref[g], 0.0).astype(BF16)
        bias_g = sbt_ref[:, g:g + 1]
        sl = slice(g * HEAD, (g + 1) * HEAD)
        blocks = []
        for c in range(rows // A_CHUNK):
            rs = slice(c * A_CHUNK, (c + 1) * A_CHUNK)
            mixed = _dot(w_g, vn_ref[rs, sl]) + bias_g
            blocks.append(u[rs, sl] * mixed)
        y_g = jnp.concatenate(blocks, axis=0)
        ya_parts.append(_rms(y_g, nao_ref[:, sl]))
    y_ref[:, 0:D_A] = jnp.concatenate(ya_parts, axis=-1).astype(BF16)

    base = 2 * D_A
    zq = z_ref[:, base:base + D_B]
    zf = z_ref[:, base + D_B:base + 2 * D_B]
    v_ref[...] = z_ref[:, base + 2 * D_B:base + 3 * D_B].astype(BF16)

    soft2 = jnp.log2(1.0 + jnp.exp2(-jnp.abs(zf) * LOG2E))
    log2_sig = jnp.minimum(zf, 0.0) * LOG2E - soft2
    if layer == 0:
        g2 = log2_sig
    else:
        lbs = lbs_ref[...]
        sm = jnp.exp(lbs - jnp.max(lbs, axis=0, keepdims=True))
        sm = sm / jnp.sum(sm, axis=0, keepdims=True)
        lb = jnp.sum(sm[1:layer + 1, :], axis=0, keepdims=True)
        t0 = jnp.log2(lb)
        t1 = jnp.log1p(-lb) * LOG2E + log2_sig
        g2 = jnp.maximum(t0, t1) + jnp.log2(1.0 + jnp.exp2(-jnp.abs(t0 - t1)))
    f = jnp.exp2(g2)
    kk = 1.0 - f

    g_hi = g2.astype(BF16)
    g_lo = (g2 - g_hi.astype(F32)).astype(BF16)
    ltri = (lax.broadcasted_iota(jnp.int32, (B_CHUNK, B_CHUNK), 0)
            >= lax.broadcasted_iota(jnp.int32, (B_CHUNK, B_CHUNK), 1)).astype(BF16)
    b_parts = []
    for c in range(rows // B_CHUNK):
        rs = slice(c * B_CHUNK, (c + 1) * B_CHUNK)
        b_parts.append(_dot(ltri, g_hi[rs]) + _dot(ltri, g_lo[rs]))
    b = jnp.concatenate(b_parts, axis=0)

    def put_kt(lvl, k_lvl):
        for hd in range(B_HEADS):
            kst_ref[lvl, hd] = k_lvl[:, hd * HEAD:(hd + 1) * HEAD].T.astype(BF16)

    for lvl in range(1, N_LEVELS):
        q_lvl, k_lvl = _level_operands(zq, kk, b, 1 << lvl)
        qs_ref[lvl - 1] = q_lvl.astype(BF16)
        put_kt(lvl - 1, k_lvl)
    qf = zq * f

    b3 = b.reshape(rows // B_CHUNK, B_CHUNK, D_B)
    b_last = b3[:, B_CHUNK - 1:B_CHUNK, :]
    qe_ref[...] = (zq * jnp.exp2(b)).astype(BF16)
    ke_ref[...] = (kk.reshape(b3.shape) * jnp.exp2(b_last - b3)).reshape(rows, D_B).astype(BF16)
    eb_ref[...] = jnp.exp2(b_last).reshape(rows // B_CHUNK, D_B)

    ri = lax.broadcasted_iota(jnp.int32, (B_CHUNK, B_CHUNK), 0)
    ci = lax.broadcasted_iota(jnp.int32, (B_CHUNK, B_CHUNK), 1)
    xor = ri ^ ci
    masks = [(ri > ci) & (xor >= (1 << lvl)) & (xor < (2 << lvl)) for lvl in range(N_LEVELS)]
    masks.append(ri == ci)

    for c in range(rows // B_CHUNK):
        rs = slice(c * B_CHUNK, (c + 1) * B_CHUNK)
        decay = eb_ref[c:c + 1, :]
        for hd in range(B_HEADS):
            sl = slice(hd * HEAD, (hd + 1) * HEAD)
            kc = kk[rs, sl]
            k_prev = pltpu.roll(kc.reshape(B_CHUNK // SUBLANES, SUBLANES, HEAD), 1, 1)
            pair = jnp.sum(qf[rs, sl] * k_prev.reshape(B_CHUNK, HEAD), axis=-1, keepdims=True)
            diag = jnp.sum(zq[rs, sl] * kc, axis=-1, keepdims=True)
            scores = jnp.where(masks[N_LEVELS], diag, jnp.where(masks[0], pair, 0.0))
            for lvl in range(1, N_LEVELS):
                k_t = kst_ref[lvl - 1, hd, :, rs]
                scores = jnp.where(masks[lvl], _dot(qs_ref[lvl - 1, rs, sl], k_t), scores)
            vh = v_ref[rs, sl]
            st = st_ref[hd]
            o = _dot(scores.astype(BF16), vh) + _dot(qe_ref[rs, sl], st.T.astype(BF16))
            ob_ref[rs, sl] = o
            st_ref[hd] = st * decay[:, sl] + _dot_tn(vh, ke_ref[rs, sl])

    zg = z_ref[:, base + 3 * D_B:base + 4 * D_B]
    gate = zg / (1.0 + jnp.exp2(zg * -LOG2E))
    y_ref[:, D_A:] = (_group_rms(ob_ref[...], nbo_ref[...], B_HEADS) * gate).astype(BF16)

    o_ref[...] = x + _dot(y_ref[...], wout_ref[...])


def _mlp_kernel(final, x_ref, nw_ref, wup_ref, wdn_ref, nf_ref, o_ref):
    x = x_ref[...]
    h = _rms(x, nw_ref[...]).astype(BF16)
    acc = x
    for j in range(D_FF // FF_BLOCK):
        cs = slice(j * FF_BLOCK, (j + 1) * FF_BLOCK)
        a = jnp.maximum(_dot(h, wup_ref[:, cs].astype(BF16)), 0.0)
        acc = acc + _dot((a * a).astype(BF16), wdn_ref[cs, :].astype(BF16))
    if final:
        acc = _rms(acc, nf_ref[...])
    o_ref[...] = acc


def _full(shape):
    return pl.BlockSpec(shape, lambda *_: (0,) * len(shape))


def _mixer(layer, x, nmix, win, sw, sbt, nv, nao, lbs, nbo, wout):
    bsz, seq, _ = x.shape
    rows = MIX_ROWS
    n_chunks = rows // B_CHUNK
    xspec = pl.BlockSpec((None, rows, D_MODEL), lambda bi, si: (bi, si, 0))
    return pl.pallas_call(
        functools.partial(_mixer_kernel, layer),
        grid=(bsz, seq // rows),
        in_specs=[xspec, _full(nmix.shape),
                  pl.BlockSpec((None, D_MODEL, D_IN), lambda bi, si: (layer, 0, 0),
                               pipeline_mode=pl.Buffered(1)),
                  _full(sw.shape), _full(sbt.shape),
                  _full(nv.shape), _full(nao.shape), _full(lbs.shape), _full(nbo.shape),
                  pl.BlockSpec(wout.shape, lambda bi, si: (0, 0), pipeline_mode=pl.Buffered(1))],
        out_specs=xspec,
        out_shape=jax.ShapeDtypeStruct(x.shape, F32),
        scratch_shapes=[
            pltpu.VMEM((B_HEADS, HEAD, HEAD), F32),
            pltpu.VMEM((rows, D_IN), F32),
            pltpu.VMEM((rows, D_A), BF16),
            pltpu.VMEM((N_LEVELS - 1, rows, D_B), BF16),
            pltpu.VMEM((N_LEVELS - 1, B_HEADS, HEAD, rows), BF16),
            pltpu.VMEM((rows, D_B), BF16),
            pltpu.VMEM((rows, D_B), BF16),
            pltpu.VMEM((rows, D_B), BF16),
            pltpu.VMEM((n_chunks, D_B), F32),
            pltpu.VMEM((rows, D_B), F32),
            pltpu.VMEM((rows, D_MODEL), BF16),
        ],
        compiler_params=pltpu.CompilerParams(
            dimension_semantics=("arbitrary", "arbitrary"), vmem_limit_bytes=VMEM_LIMIT_BYTES),
        name=f"mixer{layer}",
    )(x, nmix, win, sw, sbt, nv, nao, lbs, nbo, wout)


def _mlp(layer, final, x2, nw, w_up, w_down, nf):
    n_tok = x2.shape[0]
    xspec = pl.BlockSpec((MLP_ROWS, D_MODEL), lambda i: (i, 0))
    of_layer = lambda i: (layer, 0, 0)
    return pl.pallas_call(
        functools.partial(_mlp_kernel, final),
        grid=(n_tok // MLP_ROWS,),
        in_specs=[xspec, _full(nw.shape), pl.BlockSpec((None, D_MODEL, D_FF), of_layer),
                  pl.BlockSpec((None, D_FF, D_MODEL), of_layer), _full(nf.shape)],
        out_specs=xspec,
        out_shape=jax.ShapeDtypeStruct(x2.shape, F32),
        compiler_params=pltpu.CompilerParams(
            dimension_semantics=("arbitrary",), vmem_limit_bytes=VMEM_LIMIT_BYTES),
        name=f"mlp{layer}",
    )(x2, nw, w_up, w_down, nf)


def kernel(x, norm_mix, w_in, spatial_w, spatial_b, norm_v, norm_a_out, lower_bounds, norm_b_out,
           w_out, norm_mlp, w_up, w_down, norm_final):
    bsz, seq, d = x.shape
    depth = w_in.shape[0]
    assert d == D_MODEL and seq % MIX_ROWS == 0 and (bsz * seq) % MLP_ROWS == 0
    row = lambda a: a.reshape(1, -1).astype(F32)
    for l in range(depth):
        x = _mixer(l, x, row(norm_mix[l]), w_in, spatial_w[l], spatial_b[l].T,
                   row(norm_v[l]), row(norm_a_out[l]), lower_bounds.astype(F32), row(norm_b_out[l]),
                   w_out[l].astype(BF16))
        x2 = _mlp(l, l == depth - 1, x.reshape(bsz * seq, d), row(norm_mlp[l]),
                  w_up, w_down, row(norm_final))
        x = x2.reshape(bsz, seq, d)
    return x
```

```python
import functools

import jax
import jax.numpy as jnp
from jax import lax
from jax.experimental import pallas as pl
from jax.experimental.pallas import tpu as pltpu

D_MODEL = 1024
D_A = 512
D_B = 512
A_GROUPS = 4
A_CHUNK = 128
B_HEADS = 4
HEAD = 128
B_CHUNK = 128
D_IN = 2 * D_A + 4 * D_B
D_FF = 4 * D_MODEL
EPS = 1e-6
LOG2E = 1.4426950408889634
SQRT_2_OVER_PI = 0.7978845608028654

SUBLANES = 8
PACK_ROWS = 16
N_LEVELS = 7
FULL_LEVELS = PACK_ROWS.bit_length() - 2
VMEM_LIMIT_BYTES = 63 * 1024 * 1024

MIX_ROWS = 1024
IN_BLOCK = 512
MLP_ROWS = 1024
FF_BLOCK = 1024

BF16 = jnp.bfloat16
F32 = jnp.float32


def _dot(a, b):
    return jnp.dot(a, b, preferred_element_type=F32)


def _dot_tn(a, b):
    return lax.dot_general(a, b, (((0,), (0,)), ((), ())), preferred_element_type=F32)


def _rms(x, w):
    return x * lax.rsqrt(jnp.mean(x * x, axis=-1, keepdims=True) + EPS) * w


def _group_rms(x, w, groups):
    width = x.shape[-1] // groups
    parts = []
    for g in range(groups):
        sl = slice(g * width, (g + 1) * width)
        parts.append(_rms(x[:, sl], w[:, sl]))
    return jnp.concatenate(parts, axis=-1)


def _gelu(x):
    a = -2.0 * SQRT_2_OVER_PI * LOG2E
    return x / (1.0 + jnp.exp2(x * (a + (a * 0.044715) * (x * x))))


def _level_operands(q, k, b, half):
    rows, width = b.shape
    n = 2 * half
    if half >= SUBLANES:
        shape = (rows // n, 2, half, width)
        b4, q4, k4 = b.reshape(shape), q.reshape(shape), k.reshape(shape)
        first, second = b4[:, 0], b4[:, 1]
        mid = first[:, half - 1:half, :]
        zero = jnp.zeros_like(first)
        qe = q4[:, 1] * jnp.exp2(second - mid)
        ke = jnp.stack([k4[:, 0] * jnp.exp2(mid - first), zero], axis=1)
        if half >= PACK_ROWS:
            return qe.reshape(rows // 2, width), ke.reshape(rows, width)
        return jnp.stack([zero, qe], axis=1).reshape(rows, width), ke.reshape(rows, width)
    b3 = b.reshape(rows // SUBLANES, SUBLANES, width)
    pos = lax.broadcasted_iota(jnp.int32, b3.shape, 1) % n
    if half == SUBLANES // 2:
        mid = jnp.broadcast_to(b3[:, half - 1:half, :], b3.shape)
    else:
        assert half == 2
        prev1 = pltpu.roll(b3, 1, 1)
        nxt1 = pltpu.roll(b3, SUBLANES - 1, 1)
        prev2 = pltpu.roll(b3, 2, 1)
        mid = jnp.where(pos == 0, nxt1, jnp.where(pos == 1, b3, jnp.where(pos == 2, prev1, prev2)))
    sign = jnp.where(pos < half, -1.0, 1.0)
    e = jnp.exp2((b3 - mid) * sign).reshape(rows, width)
    return q * e, k * e


def _mixer_kernel(layer, x_ref, nmix_ref, win_ref, sw_ref, sbt_ref, nv_ref, nao_ref, lbs_ref,
                  nbo_ref, wout_ref, o_ref,
                  st_ref, z_ref, vn_ref, qs_ref, qc_ref, kst_ref, qe_ref, ke_ref, v_ref, eb_ref, ob_ref, y_ref):
    rows = x_ref.shape[0]

    @pl.when(pl.program_id(1) == 0)
    def _():
        st_ref[...] = jnp.zeros_like(st_ref)

    x = x_ref[...]
    h = _rms(x, nmix_ref[...])
    hb = h.astype(BF16)
    for j in range(D_IN // IN_BLOCK):
        cs = slice(j * IN_BLOCK, (j + 1) * IN_BLOCK)
        z_ref[:, cs] = _dot(hb, win_ref[:, cs].astype(BF16))

    u = _gelu(z_ref[:, 0:D_A])
    vn_ref[...] = _group_rms(_gelu(z_ref[:, D_A:2 * D_A]), nv_ref[...], A_GROUPS).astype(BF16)
    tri = (lax.broadcasted_iota(jnp.int32, (A_CHUNK, A_CHUNK), 0)
           >= lax.broadcasted_iota(jnp.int32, (A_CHUNK, A_CHUNK), 1))
    ya_parts = []
    for g in range(A_GROUPS):
        w_g = jnp.where(tri, sw_ref[g], 0.0).astype(BF16)
        bias_g = sbt_ref[:, g:g + 1]
        sl = slice(g * HEAD, (g + 1) * HEAD)
        blocks = []
        for c in range(rows // A_CHUNK):
            rs = slice(c * A_CHUNK, (c + 1) * A_CHUNK)
            mixed = _dot(w_g, vn_ref[rs, sl]) + bias_g
            blocks.append(u[rs, sl] * mixed)
        y_g = jnp.concatenate(blocks, axis=0)
        ya_parts.append(_rms(y_g, nao_ref[:, sl]))
    y_ref[:, 0:D_A] = jnp.concatenate(ya_parts, axis=-1).astype(BF16)

    base = 2 * D_A
    zq = z_ref[:, base:base + D_B]
    zf = z_ref[:, base + D_B:base + 2 * D_B]
    v_ref[...] = z_ref[:, base + 2 * D_B:base + 3 * D_B].astype(BF16)

    soft2 = jnp.log2(1.0 + jnp.exp2(-jnp.abs(zf) * LOG2E))
    log2_sig = jnp.minimum(zf, 0.0) * LOG2E - soft2
    if layer == 0:
        g2 = log2_sig
    else:
        lbs = lbs_ref[...]
        sm = jnp.exp(lbs - jnp.max(lbs, axis=0, keepdims=True))
        sm = sm / jnp.sum(sm, axis=0, keepdims=True)
        lb = jnp.sum(sm[1:layer + 1, :], axis=0, keepdims=True)
        t0 = jnp.log2(lb)
        t1 = jnp.log1p(-lb) * LOG2E + log2_sig
        g2 = jnp.maximum(t0, t1) + jnp.log2(1.0 + jnp.exp2(-jnp.abs(t0 - t1)))
    f = jnp.exp2(g2)
    kk = 1.0 - f

    g_hi = g2.astype(BF16)
    g_lo = (g2 - g_hi.astype(F32)).astype(BF16)
    ltri = (lax.broadcasted_iota(jnp.int32, (B_CHUNK, B_CHUNK), 0)
            >= lax.broadcasted_iota(jnp.int32, (B_CHUNK, B_CHUNK), 1)).astype(BF16)
    b_parts = []
    for c in range(rows // B_CHUNK):
        rs = slice(c * B_CHUNK, (c + 1) * B_CHUNK)
        b_parts.append(_dot(ltri, g_hi[rs]) + _dot(ltri, g_lo[rs]))
    b = jnp.concatenate(b_parts, axis=0)

    def put_kt(lvl, k_lvl):
        for hd in range(B_HEADS):
            kst_ref[lvl, hd] = k_lvl[:, hd * HEAD:(hd + 1) * HEAD].T.astype(BF16)

    for lvl in range(1, N_LEVELS):
        q_lvl, k_lvl = _level_operands(zq, kk, b, 1 << lvl)
        if lvl <= FULL_LEVELS:
            qs_ref[lvl - 1] = q_lvl.astype(BF16)
        else:
            qc_ref[lvl - 1 - FULL_LEVELS] = q_lvl.astype(BF16)
        put_kt(lvl - 1, k_lvl)
    qf = zq * f

    b3 = b.reshape(rows // B_CHUNK, B_CHUNK, D_B)
    b_last = b3[:, B_CHUNK - 1:B_CHUNK, :]
    qe_ref[...] = (zq * jnp.exp2(b)).astype(BF16)
    ke_ref[...] = (kk.reshape(b3.shape) * jnp.exp2(b_last - b3)).reshape(rows, D_B).astype(BF16)
    eb_ref[...] = jnp.exp2(b_last).reshape(rows // B_CHUNK, D_B)

    ri = lax.broadcasted_iota(jnp.int32, (B_CHUNK, B_CHUNK), 0)
    ci = lax.broadcasted_iota(jnp.int32, (B_CHUNK, B_CHUNK), 1)
    xor = ri ^ ci
    masks = [(ri > ci) & (xor >= (1 << lvl)) & (xor < (2 << lvl)) for lvl in range(N_LEVELS)]
    masks.append(ri == ci)

    for c in range(rows // B_CHUNK):
        rs = slice(c * B_CHUNK, (c + 1) * B_CHUNK)
        decay = eb_ref[c:c + 1, :]
        for hd in range(B_HEADS):
            sl = slice(hd * HEAD, (hd + 1) * HEAD)
            kc = kk[rs, sl]
            k_prev = pltpu.roll(kc.reshape(B_CHUNK // SUBLANES, SUBLANES, HEAD), 1, 1)
            pair = jnp.sum(qf[rs, sl] * k_prev.reshape(B_CHUNK, HEAD), axis=-1, keepdims=True)
            diag = jnp.sum(zq[rs, sl] * kc, axis=-1, keepdims=True)
            scores = jnp.where(masks[N_LEVELS], diag, jnp.where(masks[0], pair, 0.0))
            pieces = [scores[r:r + PACK_ROWS] for r in range(0, B_CHUNK, PACK_ROWS)]
            for lvl in range(1, N_LEVELS):
                half = 1 << lvl
                k_t = kst_ref[lvl - 1, hd, :, rs]
                compact = lvl > FULL_LEVELS
                if compact:
                    q_lvl = qc_ref[lvl - 1 - FULL_LEVELS, c * (B_CHUNK // 2):(c + 1) * (B_CHUNK // 2), sl]
                else:
                    q_lvl = qs_ref[lvl - 1, rs, sl]
                p = _dot(q_lvl, k_t)
                for i in range(len(pieces)):
                    r = i * PACK_ROWS
                    if compact and not r & half:
                        continue
                    src = (r // (2 * half)) * half + r % half if compact else r
                    pieces[i] = jnp.where(masks[lvl][r:r + PACK_ROWS], p[src:src + PACK_ROWS], pieces[i])
            scores = jnp.concatenate(pieces, axis=0)
            vh = v_ref[rs, sl]
            st = st_ref[hd]
            o = _dot(scores.astype(BF16), vh) + _dot(qe_ref[rs, sl], st.T.astype(BF16))
            ob_ref[rs, sl] = o
            st_ref[hd] = st * decay[:, sl] + _dot_tn(vh, ke_ref[rs, sl])

    zg = z_ref[:, base + 3 * D_B:base + 4 * D_B]
    gate = zg / (1.0 + jnp.exp2(zg * -LOG2E))
    y_ref[:, D_A:] = (_group_rms(ob_ref[...], nbo_ref[...], B_HEADS) * gate).astype(BF16)

    o_ref[...] = x + _dot(y_ref[...], wout_ref[...])


def _mlp_kernel(final, x_ref, nw_ref, wup_ref, wdn_ref, nf_ref, o_ref):
    x = x_ref[...]
    h = _rms(x, nw_ref[...]).astype(BF16)
    acc = x
    for j in range(D_FF // FF_BLOCK):
        cs = slice(j * FF_BLOCK, (j + 1) * FF_BLOCK)
        a = jnp.maximum(_dot(h, wup_ref[:, cs].astype(BF16)), 0.0)
        acc = acc + _dot((a * a).astype(BF16), wdn_ref[cs, :].astype(BF16))
    if final:
        acc = _rms(acc, nf_ref[...])
    o_ref[...] = acc


def _full(shape):
    return pl.BlockSpec(shape, lambda *_: (0,) * len(shape))


def _mixer(layer, x, nmix, win, sw, sbt, nv, nao, lbs, nbo, wout):
    bsz, seq, _ = x.shape
    rows = MIX_ROWS
    n_chunks = rows // B_CHUNK
    xspec = pl.BlockSpec((None, rows, D_MODEL), lambda bi, si: (bi, si, 0))
    return pl.pallas_call(
        functools.partial(_mixer_kernel, layer),
        grid=(bsz, seq // rows),
        in_specs=[xspec, _full(nmix.shape),
                  pl.BlockSpec((None, D_MODEL, D_IN), lambda bi, si: (layer, 0, 0),
                               pipeline_mode=pl.Buffered(1)),
                  _full(sw.shape), _full(sbt.shape),
                  _full(nv.shape), _full(nao.shape), _full(lbs.shape), _full(nbo.shape),
                  pl.BlockSpec(wout.shape, lambda bi, si: (0, 0), pipeline_mode=pl.Buffered(1))],
        out_specs=xspec,
        out_shape=jax.ShapeDtypeStruct(x.shape, F32),
        scratch_shapes=[
            pltpu.VMEM((B_HEADS, HEAD, HEAD), F32),
            pltpu.VMEM((rows, D_IN), F32),
            pltpu.VMEM((rows, D_A), BF16),
            pltpu.VMEM((FULL_LEVELS, rows, D_B), BF16),
            pltpu.VMEM((N_LEVELS - 1 - FULL_LEVELS, rows // 2, D_B), BF16),
            pltpu.VMEM((N_LEVELS - 1, B_HEADS, HEAD, rows), BF16),
            pltpu.VMEM((rows, D_B), BF16),
            pltpu.VMEM((rows, D_B), BF16),
            pltpu.VMEM((rows, D_B), BF16),
            pltpu.VMEM((n_chunks, D_B), F32),
            pltpu.VMEM((rows, D_B), F32),
            pltpu.VMEM((rows, D_MODEL), BF16),
        ],
        compiler_params=pltpu.CompilerParams(
            dimension_semantics=("arbitrary", "arbitrary"), vmem_limit_bytes=VMEM_LIMIT_BYTES),
        name=f"mixer{layer}",
    )(x, nmix, win, sw, sbt, nv, nao, lbs, nbo, wout)


def _mlp(layer, final, x2, nw, w_up, w_down, nf):
    n_tok = x2.shape[0]
    xspec = pl.BlockSpec((MLP_ROWS, D_MODEL), lambda i: (i, 0))
    of_layer = lambda i: (layer, 0, 0)
    return pl.pallas_call(
        functools.partial(_mlp_kernel, final),
        grid=(n_tok // MLP_ROWS,),
        in_specs=[xspec, _full(nw.shape), pl.BlockSpec((None, D_MODEL, D_FF), of_layer),
                  pl.BlockSpec((None, D_FF, D_MODEL), of_layer), _full(nf.shape)],
        out_specs=xspec,
        out_shape=jax.ShapeDtypeStruct(x2.shape, F32),
        compiler_params=pltpu.CompilerParams(
            dimension_semantics=("arbitrary",), vmem_limit_bytes=VMEM_LIMIT_BYTES),
        name=f"mlp{layer}",
    )(x2, nw, w_up, w_down, nf)


def kernel(x, norm_mix, w_in, spatial_w, spatial_b, norm_v, norm_a_out, lower_bounds, norm_b_out,
           w_out, norm_mlp, w_up, w_down, norm_final):
    bsz, seq, d = x.shape
    depth = w_in.shape[0]
    assert d == D_MODEL and seq % MIX_ROWS == 0 and (bsz * seq) % MLP_ROWS == 0
    row = lambda a: a.reshape(1, -1).astype(F32)
    for l in range(depth):
        x = _mixer(l, x, row(norm_mix[l]), w_in, spatial_w[l], spatial_b[l].T,
                   row(norm_v[l]), row(norm_a_out[l]), lower_bounds.astype(F32), row(norm_b_out[l]),
                   w_out[l].astype(BF16))
        x2 = _mlp(l, l == depth - 1, x.reshape(bsz * seq, d), row(norm_mlp[l]),
                  w_up, w_down, row(norm_final))
        x = x2.reshape(bsz, seq, d)
    return x
```

```python
import functools

import jax
import jax.numpy as jnp
from jax import lax
from jax.experimental import pallas as pl
from jax.experimental.pallas import tpu as pltpu

D_MODEL = 1024
D_A = 512
D_B = 512
A_GROUPS = 4
A_CHUNK = 128
B_HEADS = 4
HEAD = 128
B_CHUNK = 128
D_IN = 2 * D_A + 4 * D_B
D_FF = 4 * D_MODEL
EPS = 1e-6
LOG2E = 1.4426950408889634
SQRT_2_OVER_PI = 0.7978845608028654

SUBLANES = 8
PACK_ROWS = 16
N_LEVELS = 7
FULL_LEVELS = PACK_ROWS.bit_length() - 2
VMEM_LIMIT_BYTES = 63 * 1024 * 1024

MIX_ROWS = 1024
IN_BLOCK = 512
MLP_ROWS = 1024
FF_BLOCK = 1024

BF16 = jnp.bfloat16
F32 = jnp.float32


def _dot(a, b):
    return jnp.dot(a, b, preferred_element_type=F32)


def _dot_tn(a, b):
    return lax.dot_general(a, b, (((0,), (0,)), ((), ())), preferred_element_type=F32)


def _rms(x, w):
    return x * lax.rsqrt(jnp.mean(x * x, axis=-1, keepdims=True) + EPS) * w


def _group_rms(x, w, groups):
    width = x.shape[-1] // groups
    parts = []
    for g in range(groups):
        sl = slice(g * width, (g + 1) * width)
        parts.append(_rms(x[:, sl], w[:, sl]))
    return jnp.concatenate(parts, axis=-1)


def _gelu(x):
    a = -2.0 * SQRT_2_OVER_PI * LOG2E
    return x / (1.0 + jnp.exp2(x * (a + (a * 0.044715) * (x * x))))


def _level_operands(q, k, b, half):
    rows, width = b.shape
    n = 2 * half
    if half >= SUBLANES:
        shape = (rows // n, 2, half, width)
        b4, q4, k4 = b.reshape(shape), q.reshape(shape), k.reshape(shape)
        first, second = b4[:, 0], b4[:, 1]
        mid = first[:, half - 1:half, :]
        zero = jnp.zeros_like(first)
        qe = q4[:, 1] * jnp.exp2(second - mid)
        ke = jnp.stack([k4[:, 0] * jnp.exp2(mid - first), zero], axis=1)
        if half >= PACK_ROWS:
            return qe.reshape(rows // 2, width), ke.reshape(rows, width)
        return jnp.stack([zero, qe], axis=1).reshape(rows, width), ke.reshape(rows, width)
    b3 = b.reshape(rows // SUBLANES, SUBLANES, width)
    pos = lax.broadcasted_iota(jnp.int32, b3.shape, 1) % n
    if half == SUBLANES // 2:
        mid = jnp.broadcast_to(b3[:, half - 1:half, :], b3.shape)
    else:
        assert half == 2
        prev1 = pltpu.roll(b3, 1, 1)
        nxt1 = pltpu.roll(b3, SUBLANES - 1, 1)
        prev2 = pltpu.roll(b3, 2, 1)
        mid = jnp.where(pos == 0, nxt1, jnp.where(pos == 1, b3, jnp.where(pos == 2, prev1, prev2)))
    sign = jnp.where(pos < half, -1.0, 1.0)
    e = jnp.exp2((b3 - mid) * sign).reshape(rows, width)
    return q * e, k * e


def _mixer_kernel(layer, x_ref, nmix_ref, win_ref, sw_ref, sbt_ref, nv_ref, nao_ref, lbs_ref,
                  nbo_ref, wout_ref, o_ref,
                  st_ref, z_ref, vn_ref, qs_ref, qc_ref, kst_ref, qe_ref, ke_ref, v_ref, eb_ref, ob_ref, y_ref):
    rows = x_ref.shape[0]

    @pl.when(pl.program_id(1) == 0)
    def _():
        st_ref[...] = jnp.zeros_like(st_ref)

    x = x_ref[...]
    h = _rms(x, nmix_ref[...])
    hb = h.astype(BF16)
    for j in range(D_IN // IN_BLOCK):
        cs = slice(j * IN_BLOCK, (j + 1) * IN_BLOCK)
        z_ref[:, cs] = _dot(hb, win_ref[:, cs].astype(BF16))

    u = _gelu(z_ref[:, 0:D_A])
    vn_ref[...] = _group_rms(_gelu(z_ref[:, D_A:2 * D_A]), nv_ref[...], A_GROUPS).astype(BF16)
    tri = (lax.broadcasted_iota(jnp.int32, (A_CHUNK, A_CHUNK), 0)
           >= lax.broadcasted_iota(jnp.int32, (A_CHUNK, A_CHUNK), 1))
    ya_parts = []
    for g in range(A_GROUPS):
        w_g = jnp.where(tri, sw_ref[g], 0.0).astype(BF16)
        bias_g = sbt_ref[:, g:g + 1]
        sl = slice(g * HEAD, (g + 1) * HEAD)
        blocks = []
        for c in range(rows // A_CHUNK):
            rs = slice(c * A_CHUNK, (c + 1) * A_CHUNK)
            mixed = _dot(w_g, vn_ref[rs, sl]) + bias_g
            blocks.append(u[rs, sl] * mixed)
        y_g = jnp.concatenate(blocks, axis=0)
        ya_parts.append(_rms(y_g, nao_ref[:, sl]))
    y_ref[:, 0:D_A] = jnp.concatenate(ya_parts, axis=-1).astype(BF16)

    base = 2 * D_A
    zq = z_ref[:, base:base + D_B]
    zf = z_ref[:, base + D_B:base + 2 * D_B]
    v_ref[...] = z_ref[:, base + 2 * D_B:base + 3 * D_B].astype(BF16)

    soft2 = jnp.log2(1.0 + jnp.exp2(-jnp.abs(zf) * LOG2E))
    log2_sig = jnp.minimum(zf, 0.0) * LOG2E - soft2
    if layer == 0:
        g2 = log2_sig
    else:
        lbs = lbs_ref[...]
        sm = jnp.exp(lbs - jnp.max(lbs, axis=0, keepdims=True))
        sm = sm / jnp.sum(sm, axis=0, keepdims=True)
        lb = jnp.sum(sm[1:layer + 1, :], axis=0, keepdims=True)
        t0 = jnp.log2(lb)
        t1 = jnp.log1p(-lb) * LOG2E + log2_sig
        g2 = jnp.maximum(t0, t1) + jnp.log2(1.0 + jnp.exp2(-jnp.abs(t0 - t1)))
    f = jnp.exp2(g2)
    kk = 1.0 - f

    g_hi = g2.astype(BF16)
    g_lo = (g2 - g_hi.astype(F32)).astype(BF16)
    ltri = (lax.broadcasted_iota(jnp.int32, (B_CHUNK, B_CHUNK), 0)
            >= lax.broadcasted_iota(jnp.int32, (B_CHUNK, B_CHUNK), 1)).astype(BF16)
    b_parts = []
    for c in range(rows // B_CHUNK):
        rs = slice(c * B_CHUNK, (c + 1) * B_CHUNK)
        b_parts.append(_dot(ltri, g_hi[rs]) + _dot(ltri, g_lo[rs]))
    b = jnp.concatenate(b_parts, axis=0)

    def put_kt(lvl, k_lvl):
        for hd in range(B_HEADS):
            kst_ref[lvl, hd] = k_lvl[:, hd * HEAD:(hd + 1) * HEAD].astype(BF16).T

    for lvl in range(1, N_LEVELS):
        q_lvl, k_lvl = _level_operands(zq, kk, b, 1 << lvl)
        if lvl <= FULL_LEVELS:
            qs_ref[lvl - 1] = q_lvl.astype(BF16)
        else:
            qc_ref[lvl - 1 - FULL_LEVELS] = q_lvl.astype(BF16)
        put_kt(lvl - 1, k_lvl)
    qf = zq * f

    b3 = b.reshape(rows // B_CHUNK, B_CHUNK, D_B)
    b_last = b3[:, B_CHUNK - 1:B_CHUNK, :]
    qe_ref[...] = (zq * jnp.exp2(b)).astype(BF16)
    ke_ref[...] = (kk.reshape(b3.shape) * jnp.exp2(b_last - b3)).reshape(rows, D_B).astype(BF16)
    eb_ref[...] = jnp.exp2(b_last).reshape(rows // B_CHUNK, D_B)

    ri = lax.broadcasted_iota(jnp.int32, (B_CHUNK, B_CHUNK), 0)
    ci = lax.broadcasted_iota(jnp.int32, (B_CHUNK, B_CHUNK), 1)
    xor = ri ^ ci
    masks = [(ri > ci) & (xor >= (1 << lvl)) & (xor < (2 << lvl)) for lvl in range(N_LEVELS)]
    masks.append(ri == ci)

    for c in range(rows // B_CHUNK):
        rs = slice(c * B_CHUNK, (c + 1) * B_CHUNK)
        decay = eb_ref[c:c + 1, :]
        for hd in range(B_HEADS):
            sl = slice(hd * HEAD, (hd + 1) * HEAD)
            kc = kk[rs, sl]
            k_prev = pltpu.roll(kc.reshape(B_CHUNK // SUBLANES, SUBLANES, HEAD), 1, 1)
            pair = jnp.sum(qf[rs, sl] * k_prev.reshape(B_CHUNK, HEAD), axis=-1, keepdims=True)
            diag = jnp.sum(zq[rs, sl] * kc, axis=-1, keepdims=True)
            scores = jnp.where(masks[N_LEVELS], diag, jnp.where(masks[0], pair, 0.0))
            pieces = [scores[r:r + PACK_ROWS] for r in range(0, B_CHUNK, PACK_ROWS)]
            for lvl in range(1, N_LEVELS):
                half = 1 << lvl
                k_t = kst_ref[lvl - 1, hd, :, rs]
                compact = lvl > FULL_LEVELS
                if compact:
                    q_lvl = qc_ref[lvl - 1 - FULL_LEVELS, c * (B_CHUNK // 2):(c + 1) * (B_CHUNK // 2), sl]
                else:
                    q_lvl = qs_ref[lvl - 1, rs, sl]
                p = _dot(q_lvl, k_t)
                for i in range(len(pieces)):
                    r = i * PACK_ROWS
                    if compact and not r & half:
                        continue
                    src = (r // (2 * half)) * half + r % half if compact else r
                    pieces[i] = jnp.where(masks[lvl][r:r + PACK_ROWS], p[src:src + PACK_ROWS], pieces[i])
            scores = jnp.concatenate(pieces, axis=0)
            vh = v_ref[rs, sl]
            st = st_ref[hd]
            o = _dot(scores.astype(BF16), vh) + _dot(qe_ref[rs, sl], st.T.astype(BF16))
            ob_ref[rs, sl] = o
            st_ref[hd] = st * decay[:, sl] + _dot_tn(vh, ke_ref[rs, sl])

    zg = z_ref[:, base + 3 * D_B:base + 4 * D_B]
    gate = zg / (1.0 + jnp.exp2(zg * -LOG2E))
    y_ref[:, D_A:] = (_group_rms(ob_ref[...], nbo_ref[...], B_HEADS) * gate).astype(BF16)

    o_ref[...] = x + _dot(y_ref[...], wout_ref[...])


def _mlp_kernel(final, x_ref, nw_ref, wup_ref, wdn_ref, nf_ref, o_ref):
    x = x_ref[...]
    h = _rms(x, nw_ref[...]).astype(BF16)
    acc = x
    for j in range(D_FF // FF_BLOCK):
        cs = slice(j * FF_BLOCK, (j + 1) * FF_BLOCK)
        a = jnp.maximum(_dot(h, wup_ref[:, cs].astype(BF16)), 0.0)
        acc = acc + _dot((a * a).astype(BF16), wdn_ref[cs, :].astype(BF16))
    if final:
        acc = _rms(acc, nf_ref[...])
    o_ref[...] = acc


def _full(shape):
    return pl.BlockSpec(shape, lambda *_: (0,) * len(shape))


def _mixer(layer, x, nmix, win, sw, sbt, nv, nao, lbs, nbo, wout):
    bsz, seq, _ = x.shape
    rows = MIX_ROWS
    n_chunks = rows // B_CHUNK
    xspec = pl.BlockSpec((None, rows, D_MODEL), lambda bi, si: (bi, si, 0))
    return pl.pallas_call(
        functools.partial(_mixer_kernel, layer),
        grid=(bsz, seq // rows),
        in_specs=[xspec, _full(nmix.shape),
                  pl.BlockSpec((None, D_MODEL, D_IN), lambda bi, si: (layer, 0, 0),
                               pipeline_mode=pl.Buffered(1)),
                  _full(sw.shape), _full(sbt.shape),
                  _full(nv.shape), _full(nao.shape), _full(lbs.shape), _full(nbo.shape),
                  pl.BlockSpec(wout.shape, lambda bi, si: (0, 0), pipeline_mode=pl.Buffered(1))],
        out_specs=xspec,
        out_shape=jax.ShapeDtypeStruct(x.shape, F32),
        scratch_shapes=[
            pltpu.VMEM((B_HEADS, HEAD, HEAD), F32),
            pltpu.VMEM((rows, D_IN), F32),
            pltpu.VMEM((rows, D_A), BF16),
            pltpu.VMEM((FULL_LEVELS, rows, D_B), BF16),
            pltpu.VMEM((N_LEVELS - 1 - FULL_LEVELS, rows // 2, D_B), BF16),
            pltpu.VMEM((N_LEVELS - 1, B_HEADS, HEAD, rows), BF16),
            pltpu.VMEM((rows, D_B), BF16),
            pltpu.VMEM((rows, D_B), BF16),
            pltpu.VMEM((rows, D_B), BF16),
            pltpu.VMEM((n_chunks, D_B), F32),
            pltpu.VMEM((rows, D_B), F32),
            pltpu.VMEM((rows, D_MODEL), BF16),
        ],
        compiler_params=pltpu.CompilerParams(
            dimension_semantics=("arbitrary", "arbitrary"), vmem_limit_bytes=VMEM_LIMIT_BYTES),
        name=f"mixer{layer}",
    )(x, nmix, win, sw, sbt, nv, nao, lbs, nbo, wout)


def _mlp(layer, final, x2, nw, w_up, w_down, nf):
    n_tok = x2.shape[0]
    xspec = pl.BlockSpec((MLP_ROWS, D_MODEL), lambda i: (i, 0))
    of_layer = lambda i: (layer, 0, 0)
    return pl.pallas_call(
        functools.partial(_mlp_kernel, final),
        grid=(n_tok // MLP_ROWS,),
        in_specs=[xspec, _full(nw.shape), pl.BlockSpec((None, D_MODEL, D_FF), of_layer),
                  pl.BlockSpec((None, D_FF, D_MODEL), of_layer), _full(nf.shape)],
        out_specs=xspec,
        out_shape=jax.ShapeDtypeStruct(x2.shape, F32),
        compiler_params=pltpu.CompilerParams(
            dimension_semantics=("arbitrary",), vmem_limit_bytes=VMEM_LIMIT_BYTES),
        name=f"mlp{layer}",
    )(x2, nw, w_up, w_down, nf)


def kernel(x, norm_mix, w_in, spatial_w, spatial_b, norm_v, norm_a_out, lower_bounds, norm_b_out,
           w_out, norm_mlp, w_up, w_down, norm_final):
    bsz, seq, d = x.shape
    depth = w_in.shape[0]
    assert d == D_MODEL and seq % MIX_ROWS == 0 and (bsz * seq) % MLP_ROWS == 0
    row = lambda a: a.reshape(1, -1).astype(F32)
    for l in range(depth):
        x = _mixer(l, x, row(norm_mix[l]), w_in, spatial_w[l], spatial_b[l].T,
                   row(norm_v[l]), row(norm_a_out[l]), lower_bounds.astype(F32), row(norm_b_out[l]),
                   w_out[l].astype(BF16))
        x2 = _mlp(l, l == depth - 1, x.reshape(bsz * seq, d), row(norm_mlp[l]),
                  w_up, w_down, row(norm_final))
        x = x2.reshape(bsz, seq, d)
    return x
```

```python
import functools

import jax
import jax.numpy as jnp
from jax import lax
from jax.experimental import pallas as pl
from jax.experimental.pallas import tpu as pltpu

D_MODEL = 1024
D_A = 512
D_B = 512
A_GROUPS = 4
A_CHUNK = 128
B_HEADS = 4
HEAD = 128
B_CHUNK = 128
D_IN = 2 * D_A + 4 * D_B
D_FF = 4 * D_MODEL
EPS = 1e-6
LOG2E = 1.4426950408889634
SQRT_2_OVER_PI = 0.7978845608028654

SUBLANES = 8
PACK_ROWS = 16
N_LEVELS = 7
FULL_LEVELS = PACK_ROWS.bit_length() - 2
VMEM_LIMIT_BYTES = 63 * 1024 * 1024

MIX_ROWS = 1024
IN_BLOCK = 512
MLP_ROWS = 1024
FF_BLOCK = 1024

BF16 = jnp.bfloat16
F32 = jnp.float32


def _dot(a, b):
    return jnp.dot(a, b, preferred_element_type=F32)


def _dot_tn(a, b):
    return lax.dot_general(a, b, (((0,), (0,)), ((), ())), preferred_element_type=F32)


def _rms(x, w):
    return x * lax.rsqrt(jnp.mean(x * x, axis=-1, keepdims=True) + EPS) * w


def _group_rms(x, w, groups):
    width = x.shape[-1] // groups
    parts = []
    for g in range(groups):
        sl = slice(g * width, (g + 1) * width)
        parts.append(_rms(x[:, sl], w[:, sl]))
    return jnp.concatenate(parts, axis=-1)


def _gelu(x):
    a = -2.0 * SQRT_2_OVER_PI * LOG2E
    return x / (1.0 + jnp.exp2(x * (a + (a * 0.044715) * (x * x))))


def _level_operands(q, k, b, f, half):
    rows, width = b.shape
    n = 2 * half
    if half >= SUBLANES:
        shape = (rows // n, 2, half, width)
        b4, q4, k4 = b.reshape(shape), q.reshape(shape), k.reshape(shape)
        first, second = b4[:, 0], b4[:, 1]
        mid = first[:, half - 1:half, :]
        zero = jnp.zeros_like(first)
        qe = q4[:, 1] * jnp.exp2(second - mid)
        ke = jnp.stack([k4[:, 0] * jnp.exp2(mid - first), zero], axis=1)
        if half >= PACK_ROWS:
            return qe.reshape(rows // 2, width), ke.reshape(rows, width)
        return jnp.stack([zero, qe], axis=1).reshape(rows, width), ke.reshape(rows, width)
    b3 = b.reshape(rows // SUBLANES, SUBLANES, width)
    pos = lax.broadcasted_iota(jnp.int32, b3.shape, 1) % n
    if half == SUBLANES // 2:
        mid = jnp.broadcast_to(b3[:, half - 1:half, :], b3.shape)
        sign = jnp.where(pos < half, -1.0, 1.0)
        e = jnp.exp2((b3 - mid) * sign)
    else:
        assert half == 2
        f3 = f.reshape(b3.shape)
        nxt = pltpu.roll(f3, SUBLANES - 1, 1)
        prev = pltpu.roll(f3, 1, 1)
        e = jnp.where(pos == 0, nxt, jnp.where(pos == 1, 1.0, jnp.where(pos == 2, f3, f3 * prev)))
    e = e.reshape(rows, width)
    return q * e, k * e


def _mixer_kernel(layer, x_ref, nmix_ref, win_ref, sw_ref, sbt_ref, nv_ref, nao_ref, lbs_ref,
                  nbo_ref, wout_ref, o_ref,
                  st_ref, z_ref, vn_ref, qs_ref, qc_ref, kst_ref, qe_ref, ke_ref, v_ref, eb_ref, ob_ref, y_ref):
    rows = x_ref.shape[0]

    @pl.when(pl.program_id(1) == 0)
    def _():
        st_ref[...] = jnp.zeros_like(st_ref)

    x = x_ref[...]
    h = _rms(x, nmix_ref[...])
    hb = h.astype(BF16)
    for j in range(D_IN // IN_BLOCK):
        cs = slice(j * IN_BLOCK, (j + 1) * IN_BLOCK)
        z_ref[:, cs] = _dot(hb, win_ref[:, cs].astype(BF16))

    u = _gelu(z_ref[:, 0:D_A])
    vn_ref[...] = _group_rms(_gelu(z_ref[:, D_A:2 * D_A]), nv_ref[...], A_GROUPS).astype(BF16)
    tri = (lax.broadcasted_iota(jnp.int32, (A_CHUNK, A_CHUNK), 0)
           >= lax.broadcasted_iota(jnp.int32, (A_CHUNK, A_CHUNK), 1))
    ya_parts = []
    for g in range(A_GROUPS):
        w_g = jnp.where(tri, sw_ref[g], 0.0).astype(BF16)
        bias_g = sbt_ref[:, g:g + 1]
        sl = slice(g * HEAD, (g + 1) * HEAD)
        blocks = []
        for c in range(rows // A_CHUNK):
            rs = slice(c * A_CHUNK, (c + 1) * A_CHUNK)
            mixed = _dot(w_g, vn_ref[rs, sl]) + bias_g
            blocks.append(u[rs, sl] * mixed)
        y_g = jnp.concatenate(blocks, axis=0)
        ya_parts.append(_rms(y_g, nao_ref[:, sl]))
    y_ref[:, 0:D_A] = jnp.concatenate(ya_parts, axis=-1).astype(BF16)
    o_ref[...] = x + _dot(y_ref[:, 0:D_A], wout_ref[0:D_A, :])

    base = 2 * D_A
    zq = z_ref[:, base:base + D_B]
    zf = z_ref[:, base + D_B:base + 2 * D_B]
    v_ref[...] = z_ref[:, base + 2 * D_B:base + 3 * D_B].astype(BF16)

    soft2 = jnp.log2(1.0 + jnp.exp2(-jnp.abs(zf) * LOG2E))
    log2_sig = jnp.minimum(zf, 0.0) * LOG2E - soft2
    if layer == 0:
        g2 = log2_sig
    else:
        lbs = lbs_ref[...]
        sm = jnp.exp(lbs - jnp.max(lbs, axis=0, keepdims=True))
        sm = sm / jnp.sum(sm, axis=0, keepdims=True)
        lb = jnp.sum(sm[1:layer + 1, :], axis=0, keepdims=True)
        t0 = jnp.log2(lb)
        t1 = jnp.log1p(-lb) * LOG2E + log2_sig
        g2 = jnp.maximum(t0, t1) + jnp.log2(1.0 + jnp.exp2(-jnp.abs(t0 - t1)))
    f = jnp.exp2(g2)
    kk = 1.0 - f

    g_hi = g2.astype(BF16)
    g_lo = (g2 - g_hi.astype(F32)).astype(BF16)
    ltri = (lax.broadcasted_iota(jnp.int32, (B_CHUNK, B_CHUNK), 0)
            >= lax.broadcasted_iota(jnp.int32, (B_CHUNK, B_CHUNK), 1)).astype(BF16)
    b_parts = []
    for c in range(rows // B_CHUNK):
        rs = slice(c * B_CHUNK, (c + 1) * B_CHUNK)
        b_parts.append(_dot(ltri, g_hi[rs]) + _dot(ltri, g_lo[rs]))
    b = jnp.concatenate(b_parts, axis=0)

    def put_kt(lvl, k_lvl):
        for hd in range(B_HEADS):
            kst_ref[lvl, hd] = k_lvl[:, hd * HEAD:(hd + 1) * HEAD].astype(BF16).T

    for lvl in range(1, N_LEVELS):
        q_lvl, k_lvl = _level_operands(zq, kk, b, f, 1 << lvl)
        if lvl <= FULL_LEVELS:
            qs_ref[lvl - 1] = q_lvl.astype(BF16)
        else:
            qc_ref[lvl - 1 - FULL_LEVELS] = q_lvl.astype(BF16)
        put_kt(lvl - 1, k_lvl)
    qf = zq * f

    b3 = b.reshape(rows // B_CHUNK, B_CHUNK, D_B)
    b_last = b3[:, B_CHUNK - 1:B_CHUNK, :]
    qe_ref[...] = (zq * jnp.exp2(b)).astype(BF16)
    ke_ref[...] = (kk.reshape(b3.shape) * jnp.exp2(b_last - b3)).reshape(rows, D_B).astype(BF16)
    eb_ref[...] = jnp.exp2(b_last).reshape(rows // B_CHUNK, D_B)

    ri = lax.broadcasted_iota(jnp.int32, (B_CHUNK, B_CHUNK), 0)
    ci = lax.broadcasted_iota(jnp.int32, (B_CHUNK, B_CHUNK), 1)
    xor = ri ^ ci
    masks = [(ri > ci) & (xor >= (1 << lvl)) & (xor < (2 << lvl)) for lvl in range(N_LEVELS)]
    masks.append(ri == ci)

    for c in range(rows // B_CHUNK):
        rs = slice(c * B_CHUNK, (c + 1) * B_CHUNK)
        decay = eb_ref[c:c + 1, :]
        for hd in range(B_HEADS):
            sl = slice(hd * HEAD, (hd + 1) * HEAD)
            kc = kk[rs, sl]
            k_prev = pltpu.roll(kc.reshape(B_CHUNK // SUBLANES, SUBLANES, HEAD), 1, 1)
            pair = jnp.sum(qf[rs, sl] * k_prev.reshape(B_CHUNK, HEAD), axis=-1, keepdims=True)
            diag = jnp.sum(zq[rs, sl] * kc, axis=-1, keepdims=True)
            scores = jnp.where(masks[N_LEVELS], diag, jnp.where(masks[0], pair, 0.0))
            pieces = [scores[r:r + PACK_ROWS] for r in range(0, B_CHUNK, PACK_ROWS)]
            for lvl in range(1, N_LEVELS):
                half = 1 << lvl
                k_t = kst_ref[lvl - 1, hd, :, rs]
                compact = lvl > FULL_LEVELS
                if compact:
                    q_lvl = qc_ref[lvl - 1 - FULL_LEVELS, c * (B_CHUNK // 2):(c + 1) * (B_CHUNK // 2), sl]
                else:
                    q_lvl = qs_ref[lvl - 1, rs, sl]
                p = _dot(q_lvl, k_t)
                for i in range(len(pieces)):
                    r = i * PACK_ROWS
                    if compact and not r & half:
                        continue
                    src = (r // (2 * half)) * half + r % half if compact else r
                    pieces[i] = jnp.where(masks[lvl][r:r + PACK_ROWS], p[src:src + PACK_ROWS], pieces[i])
            scores = jnp.concatenate(pieces, axis=0)
            vh = v_ref[rs, sl]
            st = st_ref[hd]
            o = _dot(scores.astype(BF16), vh) + _dot(qe_ref[rs, sl], st.astype(BF16).T)
            ob_ref[rs, sl] = o
            st_ref[hd] = st * decay[:, sl] + _dot_tn(vh, ke_ref[rs, sl])

    zg = z_ref[:, base + 3 * D_B:base + 4 * D_B]
    gate = zg / (1.0 + jnp.exp2(zg * -LOG2E))
    y_ref[:, D_A:] = (_group_rms(ob_ref[...], nbo_ref[...], B_HEADS) * gate).astype(BF16)

    o_ref[...] += _dot(y_ref[:, D_A:], wout_ref[D_A:, :])


def _mlp_kernel(final, x_ref, nw_ref, wup_ref, wdn_ref, nf_ref, o_ref):
    x = x_ref[...]
    h = _rms(x, nw_ref[...]).astype(BF16)
    acc = x
    for j in range(D_FF // FF_BLOCK):
        cs = slice(j * FF_BLOCK, (j + 1) * FF_BLOCK)
        a = jnp.maximum(_dot(h, wup_ref[:, cs].astype(BF16)), 0.0)
        acc = acc + _dot((a * a).astype(BF16), wdn_ref[cs, :].astype(BF16))
    if final:
        acc = _rms(acc, nf_ref[...])
    o_ref[...] = acc


def _full(shape):
    return pl.BlockSpec(shape, lambda *_: (0,) * len(shape))


def _mixer(layer, x, nmix, win, sw, sbt, nv, nao, lbs, nbo, wout):
    bsz, seq, _ = x.shape
    rows = MIX_ROWS
    n_chunks = rows // B_CHUNK
    xspec = pl.BlockSpec((None, rows, D_MODEL), lambda bi, si: (bi, si, 0))
    return pl.pallas_call(
        functools.partial(_mixer_kernel, layer),
        grid=(bsz, seq // rows),
        in_specs=[xspec, _full(nmix.shape),
                  pl.BlockSpec((None, D_MODEL, D_IN), lambda bi, si: (layer, 0, 0),
                               pipeline_mode=pl.Buffered(1)),
                  _full(sw.shape), _full(sbt.shape),
                  _full(nv.shape), _full(nao.shape), _full(lbs.shape), _full(nbo.shape),
                  pl.BlockSpec(wout.shape, lambda bi, si: (0, 0), pipeline_mode=pl.Buffered(1))],
        out_specs=xspec,
        out_shape=jax.ShapeDtypeStruct(x.shape, F32),
        scratch_shapes=[
            pltpu.VMEM((B_HEADS, HEAD, HEAD), F32),
            pltpu.VMEM((rows, D_IN), F32),
            pltpu.VMEM((rows, D_A), BF16),
            pltpu.VMEM((FULL_LEVELS, rows, D_B), BF16),
            pltpu.VMEM((N_LEVELS - 1 - FULL_LEVELS, rows // 2, D_B), BF16),
            pltpu.VMEM((N_LEVELS - 1, B_HEADS, HEAD, rows), BF16),
            pltpu.VMEM((rows, D_B), BF16),
            pltpu.VMEM((rows, D_B), BF16),
            pltpu.VMEM((rows, D_B), BF16),
            pltpu.VMEM((n_chunks, D_B), F32),
            pltpu.VMEM((rows, D_B), F32),
            pltpu.VMEM((rows, D_MODEL), BF16),
        ],
        compiler_params=pltpu.CompilerParams(
            dimension_semantics=("arbitrary", "arbitrary"), vmem_limit_bytes=VMEM_LIMIT_BYTES),
        name=f"mixer{layer}",
    )(x, nmix, win, sw, sbt, nv, nao, lbs, nbo, wout)


def _mlp(layer, final, x2, nw, w_up, w_down, nf):
    n_tok = x2.shape[0]
    xspec = pl.BlockSpec((MLP_ROWS, D_MODEL), lambda i: (i, 0))
    of_layer = lambda i: (layer, 0, 0)
    return pl.pallas_call(
        functools.partial(_mlp_kernel, final),
        grid=(n_tok // MLP_ROWS,),
        in_specs=[xspec, _full(nw.shape), pl.BlockSpec((None, D_MODEL, D_FF), of_layer),
                  pl.BlockSpec((None, D_FF, D_MODEL), of_layer), _full(nf.shape)],
        out_specs=xspec,
        out_shape=jax.ShapeDtypeStruct(x2.shape, F32),
        compiler_params=pltpu.CompilerParams(
            dimension_semantics=("arbitrary",), vmem_limit_bytes=VMEM_LIMIT_BYTES),
        name=f"mlp{layer}",
    )(x2, nw, w_up, w_down, nf)


def kernel(x, norm_mix, w_in, spatial_w, spatial_b, norm_v, norm_a_out, lower_bounds, norm_b_out,
           w_out, norm_mlp, w_up, w_down, norm_final):
    bsz, seq, d = x.shape
    depth = w_in.shape[0]
    assert d == D_MODEL and seq % MIX_ROWS == 0 and (bsz * seq) % MLP_ROWS == 0
    row = lambda a: a.reshape(1, -1).astype(F32)
    for l in range(depth):
        x = _mixer(l, x, row(norm_mix[l]), w_in, spatial_w[l], spatial_b[l].T,
                   row(norm_v[l]), row(norm_a_out[l]), lower_bounds.astype(F32), row(norm_b_out[l]),
                   w_out[l].astype(BF16))
        x2 = _mlp(l, l == depth - 1, x.reshape(bsz * seq, d), row(norm_mlp[l]),
                  w_up, w_down, row(norm_final))
        x = x2.reshape(bsz, seq, d)
    return x
```

```python
import functools

import jax
import jax.numpy as jnp
from jax import lax
from jax.experimental import pallas as pl
from jax.experimental.pallas import tpu as pltpu

D_MODEL = 1024
D_A = 512
D_B = 512
A_GROUPS = 4
A_CHUNK = 128
B_HEADS = 4
HEAD = 128
B_CHUNK = 128
D_IN = 2 * D_A + 4 * D_B
D_FF = 4 * D_MODEL
EPS = 1e-6
LOG2E = 1.4426950408889634
SQRT_2_OVER_PI = 0.7978845608028654

SUBLANES = 8
PACK_ROWS = 16
N_LEVELS = 7
FULL_LEVELS = PACK_ROWS.bit_length() - 2
VMEM_LIMIT_BYTES = 63 * 1024 * 1024

MIX_ROWS = 1024
IN_BLOCK = 512
MLP_ROWS = 1024
FF_BLOCK = 1024

BF16 = jnp.bfloat16
F32 = jnp.float32


def _dot(a, b):
    return jnp.dot(a, b, preferred_element_type=F32)


def _dot_tn(a, b):
    return lax.dot_general(a, b, (((0,), (0,)), ((), ())), preferred_element_type=F32)


def _rms(x, w):
    return x * lax.rsqrt(jnp.mean(x * x, axis=-1, keepdims=True) + EPS) * w


def _group_rms(x, w, groups):
    width = x.shape[-1] // groups
    parts = []
    for g in range(groups):
        sl = slice(g * width, (g + 1) * width)
        parts.append(_rms(x[:, sl], w[:, sl]))
    return jnp.concatenate(parts, axis=-1)


def _gelu(x):
    a = -2.0 * SQRT_2_OVER_PI * LOG2E
    return x / (1.0 + jnp.exp2(x * (a + (a * 0.044715) * (x * x))))


def _level_operands(q, k, b, f, half):
    rows, width = b.shape
    n = 2 * half
    if half >= SUBLANES:
        shape = (rows // n, 2, half, width)
        b4, q4, k4 = b.reshape(shape), q.reshape(shape), k.reshape(shape)
        first, second = b4[:, 0], b4[:, 1]
        mid = first[:, half - 1:half, :]
        zero = jnp.zeros_like(first)
        qe = q4[:, 1] * jnp.exp2(second - mid)
        ke = jnp.stack([k4[:, 0] * jnp.exp2(mid - first), zero], axis=1)
        if half >= PACK_ROWS:
            return qe.reshape(rows // 2, width), ke.reshape(rows, width)
        return jnp.stack([zero, qe], axis=1).reshape(rows, width), ke.reshape(rows, width)
    b3 = b.reshape(rows // SUBLANES, SUBLANES, width)
    pos = lax.broadcasted_iota(jnp.int32, b3.shape, 1) % n
    if half == SUBLANES // 2:
        mid = jnp.broadcast_to(b3[:, half - 1:half, :], b3.shape)
        sign = jnp.where(pos < half, -1.0, 1.0)
        e = jnp.exp2((b3 - mid) * sign)
    else:
        assert half == 2
        f3 = f.reshape(b3.shape)
        nxt = pltpu.roll(f3, SUBLANES - 1, 1)
        prev = pltpu.roll(f3, 1, 1)
        e = jnp.where(pos == 0, nxt, jnp.where(pos == 1, 1.0, jnp.where(pos == 2, f3, f3 * prev)))
    e = e.reshape(rows, width)
    return q * e, k * e


def _mixer_kernel(layer, x_ref, nmix_ref, win_ref, sw_ref, sbt_ref, nv_ref, nao_ref, lbs_ref,
                  nbo_ref, wout_ref, o_ref,
                  st_ref, z_ref, vn_ref, qs_ref, qc_ref, kst_ref, qe_ref, ke_ref, v_ref, eb_ref, ob_ref, y_ref):
    rows = x_ref.shape[0]

    @pl.when(pl.program_id(1) == 0)
    def _():
        st_ref[...] = jnp.zeros_like(st_ref)

    x = x_ref[...]
    h = _rms(x, nmix_ref[...])
    hb = h.astype(BF16)
    for j in range(D_IN // IN_BLOCK):
        cs = slice(j * IN_BLOCK, (j + 1) * IN_BLOCK)
        z_ref[:, cs] = _dot(hb, win_ref[:, cs].astype(BF16))

    u = _gelu(z_ref[:, 0:D_A])
    vn_ref[...] = _group_rms(_gelu(z_ref[:, D_A:2 * D_A]), nv_ref[...], A_GROUPS).astype(BF16)
    tri = (lax.broadcasted_iota(jnp.int32, (A_CHUNK, A_CHUNK), 0)
           >= lax.broadcasted_iota(jnp.int32, (A_CHUNK, A_CHUNK), 1))
    ya_parts = []
    for g in range(A_GROUPS):
        w_g = jnp.where(tri, sw_ref[g], 0.0).astype(BF16)
        bias_g = sbt_ref[:, g:g + 1]
        sl = slice(g * HEAD, (g + 1) * HEAD)
        blocks = []
        for c in range(rows // A_CHUNK):
            rs = slice(c * A_CHUNK, (c + 1) * A_CHUNK)
            mixed = _dot(w_g, vn_ref[rs, sl]) + bias_g
            blocks.append(u[rs, sl] * mixed)
        y_g = jnp.concatenate(blocks, axis=0)
        ya_parts.append(_rms(y_g, nao_ref[:, sl]))
    y_ref[:, 0:D_A] = jnp.concatenate(ya_parts, axis=-1).astype(BF16)

    base = 2 * D_A
    zq = z_ref[:, base:base + D_B]
    zf = z_ref[:, base + D_B:base + 2 * D_B]
    v_ref[...] = z_ref[:, base + 2 * D_B:base + 3 * D_B].astype(BF16)

    soft2 = jnp.log2(1.0 + jnp.exp2(-jnp.abs(zf) * LOG2E))
    log2_sig = jnp.minimum(zf, 0.0) * LOG2E - soft2
    if layer == 0:
        g2 = log2_sig
    else:
        lbs = lbs_ref[...]
        sm = jnp.exp(lbs - jnp.max(lbs, axis=0, keepdims=True))
        sm = sm / jnp.sum(sm, axis=0, keepdims=True)
        lb = jnp.sum(sm[1:layer + 1, :], axis=0, keepdims=True)
        t0 = jnp.log2(lb)
        t1 = jnp.log1p(-lb) * LOG2E + log2_sig
        g2 = jnp.maximum(t0, t1) + jnp.log2(1.0 + jnp.exp2(-jnp.abs(t0 - t1)))
    f = jnp.exp2(g2)
    kk = 1.0 - f

    g_hi = g2.astype(BF16)
    g_lo = (g2 - g_hi.astype(F32)).astype(BF16)
    ltri = (lax.broadcasted_iota(jnp.int32, (B_CHUNK, B_CHUNK), 0)
            >= lax.broadcasted_iota(jnp.int32, (B_CHUNK, B_CHUNK), 1)).astype(BF16)
    b_parts = []
    for c in range(rows // B_CHUNK):
        rs = slice(c * B_CHUNK, (c + 1) * B_CHUNK)
        b_parts.append(_dot(ltri, g_hi[rs]) + _dot(ltri, g_lo[rs]))
    b = jnp.concatenate(b_parts, axis=0)

    def put_kt(lvl, k_lvl):
        for hd in range(B_HEADS):
            kst_ref[lvl, hd] = k_lvl[:, hd * HEAD:(hd + 1) * HEAD].astype(BF16).T

    for lvl in range(1, N_LEVELS):
        q_lvl, k_lvl = _level_operands(zq, kk, b, f, 1 << lvl)
        if lvl <= FULL_LEVELS:
            qs_ref[lvl - 1] = q_lvl.astype(BF16)
        else:
            qc_ref[lvl - 1 - FULL_LEVELS] = q_lvl.astype(BF16)
        put_kt(lvl - 1, k_lvl)
    qf = zq * f

    b3 = b.reshape(rows // B_CHUNK, B_CHUNK, D_B)
    b_last = b3[:, B_CHUNK - 1:B_CHUNK, :]
    qe_ref[...] = (zq * jnp.exp2(b)).astype(BF16)
    ke_ref[...] = (kk.reshape(b3.shape) * jnp.exp2(b_last - b3)).reshape(rows, D_B).astype(BF16)
    eb_ref[...] = jnp.exp2(b_last).reshape(rows // B_CHUNK, D_B)

    ri = lax.broadcasted_iota(jnp.int32, (B_CHUNK, B_CHUNK), 0)
    ci = lax.broadcasted_iota(jnp.int32, (B_CHUNK, B_CHUNK), 1)
    xor = ri ^ ci
    masks = [(ri > ci) & (xor >= (1 << lvl)) & (xor < (2 << lvl)) for lvl in range(N_LEVELS)]
    masks.append(ri == ci)

    for c in range(rows // B_CHUNK):
        rs = slice(c * B_CHUNK, (c + 1) * B_CHUNK)
        decay = eb_ref[c:c + 1, :]
        for hd in range(B_HEADS):
            sl = slice(hd * HEAD, (hd + 1) * HEAD)
            kc = kk[rs, sl]
            k_prev = pltpu.roll(kc.reshape(B_CHUNK // SUBLANES, SUBLANES, HEAD), 1, 1)
            pair = jnp.sum(qf[rs, sl] * k_prev.reshape(B_CHUNK, HEAD), axis=-1, keepdims=True)
            diag = jnp.sum(zq[rs, sl] * kc, axis=-1, keepdims=True)
            scores = jnp.where(masks[N_LEVELS], diag, jnp.where(masks[0], pair, 0.0))
            pieces = [scores[r:r + PACK_ROWS] for r in range(0, B_CHUNK, PACK_ROWS)]
            for lvl in range(1, N_LEVELS):
                half = 1 << lvl
                k_t = kst_ref[lvl - 1, hd, :, rs]
                compact = lvl > FULL_LEVELS
                if compact:
                    q_lvl = qc_ref[lvl - 1 - FULL_LEVELS, c * (B_CHUNK // 2):(c + 1) * (B_CHUNK // 2), sl]
                else:
                    q_lvl = qs_ref[lvl - 1, rs, sl]
                p = _dot(q_lvl, k_t)
                for i in range(len(pieces)):
                    r = i * PACK_ROWS
                    if compact and not r & half:
                        continue
                    src = (r // (2 * half)) * half + r % half if compact else r
                    pieces[i] = jnp.where(masks[lvl][r:r + PACK_ROWS], p[src:src + PACK_ROWS], pieces[i])
            scores = jnp.concatenate(pieces, axis=0)
            vh = v_ref[rs, sl]
            st = st_ref[hd]
            o = _dot(scores.astype(BF16), vh) + _dot(qe_ref[rs, sl], st.astype(BF16).T)
            ob_ref[rs, sl] = o
            st_ref[hd] = st * decay[:, sl] + _dot_tn(vh, ke_ref[rs, sl])

    zg = z_ref[:, base + 3 * D_B:base + 4 * D_B]
    gate = zg / (1.0 + jnp.exp2(zg * -LOG2E))
    y_ref[:, D_A:] = (_group_rms(ob_ref[...], nbo_ref[...], B_HEADS) * gate).astype(BF16)

    o_ref[...] = x + _dot(y_ref[...], wout_ref[...])


def _mlp_kernel(final, x_ref, nw_ref, wup_ref, wdn_ref, nf_ref, o_ref):
    x = x_ref[...]
    h = _rms(x, nw_ref[...]).astype(BF16)
    acc = x
    for j in range(D_FF // FF_BLOCK):
        cs = slice(j * FF_BLOCK, (j + 1) * FF_BLOCK)
        a = jnp.maximum(_dot(h, wup_ref[:, cs].astype(BF16)), 0.0)
        acc = acc + _dot((a * a).astype(BF16), wdn_ref[cs, :].astype(BF16))
    if final:
        acc = _rms(acc, nf_ref[...])
    o_ref[...] = acc


def _full(shape):
    return pl.BlockSpec(shape, lambda *_: (0,) * len(shape))


def _mixer(layer, x, nmix, win, sw, sbt, nv, nao, lbs, nbo, wout):
    bsz, seq, _ = x.shape
    rows = MIX_ROWS
    n_chunks = rows // B_CHUNK
    xspec = pl.BlockSpec((None, rows, D_MODEL), lambda bi, si: (bi, si, 0))
    return pl.pallas_call(
        functools.partial(_mixer_kernel, layer),
        grid=(bsz, seq // rows),
        in_specs=[xspec, _full(nmix.shape),
                  pl.BlockSpec((None, D_MODEL, D_IN), lambda bi, si: (layer, 0, 0),
                               pipeline_mode=pl.Buffered(1)),
                  _full(sw.shape), _full(sbt.shape),
                  _full(nv.shape), _full(nao.shape), _full(lbs.shape), _full(nbo.shape),
                  pl.BlockSpec(wout.shape, lambda bi, si: (0, 0), pipeline_mode=pl.Buffered(1))],
        out_specs=xspec,
        out_shape=jax.ShapeDtypeStruct(x.shape, F32),
        scratch_shapes=[
            pltpu.VMEM((B_HEADS, HEAD, HEAD), F32),
            pltpu.VMEM((rows, D_IN), F32),
            pltpu.VMEM((rows, D_A), BF16),
            pltpu.VMEM((FULL_LEVELS, rows, D_B), BF16),
            pltpu.VMEM((N_LEVELS - 1 - FULL_LEVELS, rows // 2, D_B), BF16),
            pltpu.VMEM((N_LEVELS - 1, B_HEADS, HEAD, rows), BF16),
            pltpu.VMEM((rows, D_B), BF16),
            pltpu.VMEM((rows, D_B), BF16),
            pltpu.VMEM((rows, D_B), BF16),
            pltpu.VMEM((n_chunks, D_B), F32),
            pltpu.VMEM((rows, D_B), F32),
            pltpu.VMEM((rows, D_MODEL), BF16),
        ],
        compiler_params=pltpu.CompilerParams(
            dimension_semantics=("arbitrary", "arbitrary"), vmem_limit_bytes=VMEM_LIMIT_BYTES),
        name=f"mixer{layer}",
    )(x, nmix, win, sw, sbt, nv, nao, lbs, nbo, wout)


def _mlp(layer, final, x2, nw, w_up, w_down, nf):
    n_tok = x2.shape[0]
    xspec = pl.BlockSpec((MLP_ROWS, D_MODEL), lambda i: (i, 0))
    of_layer = lambda i: (layer, 0, 0)
    return pl.pallas_call(
        functools.partial(_mlp_kernel, final),
        grid=(n_tok // MLP_ROWS,),
        in_specs=[xspec, _full(nw.shape), pl.BlockSpec((None, D_MODEL, D_FF), of_layer),
                  pl.BlockSpec((None, D_FF, D_MODEL), of_layer), _full(nf.shape)],
        out_specs=xspec,
        out_shape=jax.ShapeDtypeStruct(x2.shape, F32),
        compiler_params=pltpu.CompilerParams(
            dimension_semantics=("arbitrary",), vmem_limit_bytes=VMEM_LIMIT_BYTES),
        name=f"mlp{layer}",
    )(x2, nw, w_up, w_down, nf)


def kernel(x, norm_mix, w_in, spatial_w, spatial_b, norm_v, norm_a_out, lower_bounds, norm_b_out,
           w_out, norm_mlp, w_up, w_down, norm_final):
    bsz, seq, d = x.shape
    depth = w_in.shape[0]
    assert d == D_MODEL and seq % MIX_ROWS == 0 and (bsz * seq) % MLP_ROWS == 0
    row = lambda a: a.reshape(1, -1).astype(F32)
    for l in range(depth):
        x = _mixer(l, x, row(norm_mix[l]), w_in, spatial_w[l], spatial_b[l].T,
                   row(norm_v[l]), row(norm_a_out[l]), lower_bounds.astype(F32), row(norm_b_out[l]),
                   w_out[l].astype(BF16))
        x2 = _mlp(l, l == depth - 1, x.reshape(bsz * seq, d), row(norm_mlp[l]),
                  w_up, w_down, row(norm_final))
        x = x2.reshape(bsz, seq, d)
    return x
```

```python
import functools

import jax
import jax.numpy as jnp
from jax import lax
from jax.experimental import pallas as pl
from jax.experimental.pallas import tpu as pltpu

D_MODEL = 1024
D_A = 512
D_B = 512
A_GROUPS = 4
A_CHUNK = 128
B_HEADS = 4
HEAD = 128
B_CHUNK = 128
D_IN = 2 * D_A + 4 * D_B
D_FF = 4 * D_MODEL
EPS = 1e-6
LOG2E = 1.4426950408889634
SQRT_2_OVER_PI = 0.7978845608028654

SUBLANES = 8
PACK_ROWS = 16
N_LEVELS = 7
FULL_LEVELS = PACK_ROWS.bit_length() - 2
VMEM_LIMIT_BYTES = 63 * 1024 * 1024

MIX_ROWS = 1024
IN_BLOCK = 512
MLP_ROWS = 1024
FF_BLOCK = 1024

BF16 = jnp.bfloat16
F32 = jnp.float32


def _dot(a, b):
    return jnp.dot(a, b, preferred_element_type=F32)


def _dot_tn(a, b):
    return lax.dot_general(a, b, (((0,), (0,)), ((), ())), preferred_element_type=F32)


def _rms(x, w):
    return x * lax.rsqrt(jnp.mean(x * x, axis=-1, keepdims=True) + EPS) * w


def _group_rms(x, w, groups):
    width = x.shape[-1] // groups
    parts = []
    for g in range(groups):
        sl = slice(g * width, (g + 1) * width)
        parts.append(_rms(x[:, sl], w[:, sl]))
    return jnp.concatenate(parts, axis=-1)


def _gelu(x):
    a = -2.0 * SQRT_2_OVER_PI * LOG2E
    return x / (1.0 + jnp.exp2(x * (a + (a * 0.044715) * (x * x))))


def _level_operands(q, k, b, half):
    rows, width = b.shape
    n = 2 * half
    if half >= SUBLANES:
        shape = (rows // n, 2, half, width)
        b4, q4, k4 = b.reshape(shape), q.reshape(shape), k.reshape(shape)
        first, second = b4[:, 0], b4[:, 1]
        mid = first[:, half - 1:half, :]
        zero = jnp.zeros_like(first)
        qe = q4[:, 1] * jnp.exp2(second - mid)
        ke = jnp.stack([k4[:, 0] * jnp.exp2(mid - first), zero], axis=1)
        if half >= PACK_ROWS:
            return qe.reshape(rows // 2, width), ke.reshape(rows, width)
        return jnp.stack([zero, qe], axis=1).reshape(rows, width), ke.reshape(rows, width)
    b3 = b.reshape(rows // SUBLANES, SUBLANES, width)
    pos = lax.broadcasted_iota(jnp.int32, b3.shape, 1) % n
    if half == SUBLANES // 2:
        mid = jnp.broadcast_to(b3[:, half - 1:half, :], b3.shape)
    else:
        assert half == 2
        prev1 = pltpu.roll(b3, 1, 1)
        nxt1 = pltpu.roll(b3, SUBLANES - 1, 1)
        prev2 = pltpu.roll(b3, 2, 1)
        mid = jnp.where(pos == 0, nxt1, jnp.where(pos == 1, b3, jnp.where(pos == 2, prev1, prev2)))
    sign = jnp.where(pos < half, -1.0, 1.0)
    e = jnp.exp2((b3 - mid) * sign).reshape(rows, width)
    return q * e, k * e


def _mixer_kernel(layer, x_ref, nmix_ref, win_ref, sw_ref, sbt_ref, nv_ref, nao_ref, lbs_ref,
                  nbo_ref, wout_ref, o_ref,
                  st_ref, z_ref, vn_ref, qs_ref, qc_ref, kst_ref, qe_ref, ke_ref, v_ref, eb_ref, ob_ref, y_ref):
    rows = x_ref.shape[0]

    @pl.when(pl.program_id(1) == 0)
    def _():
        st_ref[...] = jnp.zeros_like(st_ref)

    x = x_ref[...]
    h = _rms(x, nmix_ref[...])
    hb = h.astype(BF16)
    for j in range(D_IN // IN_BLOCK):
        cs = slice(j * IN_BLOCK, (j + 1) * IN_BLOCK)
        z_ref[:, cs] = _dot(hb, win_ref[:, cs].astype(BF16))

    u = _gelu(z_ref[:, 0:D_A])
    vn_ref[...] = _group_rms(_gelu(z_ref[:, D_A:2 * D_A]), nv_ref[...], A_GROUPS).astype(BF16)
    tri = (lax.broadcasted_iota(jnp.int32, (A_CHUNK, A_CHUNK), 0)
           >= lax.broadcasted_iota(jnp.int32, (A_CHUNK, A_CHUNK), 1))
    ya_parts = []
    for g in range(A_GROUPS):
        w_g = jnp.where(tri, sw_ref[g], 0.0).astype(BF16)
        bias_g = sbt_ref[:, g:g + 1]
        sl = slice(g * HEAD, (g + 1) * HEAD)
        blocks = []
        for c in range(rows // A_CHUNK):
            rs = slice(c * A_CHUNK, (c + 1) * A_CHUNK)
            mixed = _dot(w_g, vn_ref[rs, sl]) + bias_g
            blocks.append(u[rs, sl] * mixed)
        y_g = jnp.concatenate(blocks, axis=0)
        ya_parts.append(_rms(y_g, nao_ref[:, sl]))
    y_ref[:, 0:D_A] = jnp.concatenate(ya_parts, axis=-1).astype(BF16)

    base = 2 * D_A
    zq = z_ref[:, base:base + D_B]
    zf = z_ref[:, base + D_B:base + 2 * D_B]
    v_ref[...] = z_ref[:, base + 2 * D_B:base + 3 * D_B].astype(BF16)

    soft2 = jnp.log2(1.0 + jnp.exp2(-jnp.abs(zf) * LOG2E))
    log2_sig = jnp.minimum(zf, 0.0) * LOG2E - soft2
    if layer == 0:
        g2 = log2_sig
    else:
        lbs = lbs_ref[...]
        sm = jnp.exp(lbs - jnp.max(lbs, axis=0, keepdims=True))
        sm = sm / jnp.sum(sm, axis=0, keepdims=True)
        lb = jnp.sum(sm[1:layer + 1, :], axis=0, keepdims=True)
        t0 = jnp.log2(lb)
        t1 = jnp.log1p(-lb) * LOG2E + log2_sig
        g2 = jnp.maximum(t0, t1) + jnp.log2(1.0 + jnp.exp2(-jnp.abs(t0 - t1)))
    f = jnp.exp2(g2)
    kk = 1.0 - f

    g_hi = g2.astype(BF16)
    g_lo = (g2 - g_hi.astype(F32)).astype(BF16)
    ltri = (lax.broadcasted_iota(jnp.int32, (B_CHUNK, B_CHUNK), 0)
            >= lax.broadcasted_iota(jnp.int32, (B_CHUNK, B_CHUNK), 1)).astype(BF16)
    b_parts = []
    for c in range(rows // B_CHUNK):
        rs = slice(c * B_CHUNK, (c + 1) * B_CHUNK)
        b_parts.append(_dot(ltri, g_hi[rs]) + _dot(ltri, g_lo[rs]))
    b = jnp.concatenate(b_parts, axis=0)

    def put_kt(lvl, k_lvl):
        for hd in range(B_HEADS):
            kst_ref[lvl, hd] = k_lvl[:, hd * HEAD:(hd + 1) * HEAD].astype(BF16).T

    for lvl in range(1, N_LEVELS):
        q_lvl, k_lvl = _level_operands(zq, kk, b, 1 << lvl)
        if lvl <= FULL_LEVELS:
            qs_ref[lvl - 1] = q_lvl.astype(BF16)
        else:
            qc_ref[lvl - 1 - FULL_LEVELS] = q_lvl.astype(BF16)
        put_kt(lvl - 1, k_lvl)
    qf = zq * f

    b3 = b.reshape(rows // B_CHUNK, B_CHUNK, D_B)
    b_last = b3[:, B_CHUNK - 1:B_CHUNK, :]
    qe_ref[...] = (zq * jnp.exp2(b)).astype(BF16)
    ke_ref[...] = (kk.reshape(b3.shape) * jnp.exp2(b_last - b3)).reshape(rows, D_B).astype(BF16)
    eb_ref[...] = jnp.exp2(b_last).reshape(rows // B_CHUNK, D_B)

    ri = lax.broadcasted_iota(jnp.int32, (B_CHUNK, B_CHUNK), 0)
    ci = lax.broadcasted_iota(jnp.int32, (B_CHUNK, B_CHUNK), 1)
    xor = ri ^ ci
    masks = [(ri > ci) & (xor >= (1 << lvl)) & (xor < (2 << lvl)) for lvl in range(N_LEVELS)]
    masks.append(ri == ci)

    for c in range(rows // B_CHUNK):
        rs = slice(c * B_CHUNK, (c + 1) * B_CHUNK)
        decay = eb_ref[c:c + 1, :]
        for hd in range(B_HEADS):
            sl = slice(hd * HEAD, (hd + 1) * HEAD)
            kc = kk[rs, sl]
            k_prev = pltpu.roll(kc.reshape(B_CHUNK // SUBLANES, SUBLANES, HEAD), 1, 1)
            pair = jnp.sum(qf[rs, sl] * k_prev.reshape(B_CHUNK, HEAD), axis=-1, keepdims=True)
            diag = jnp.sum(zq[rs, sl] * kc, axis=-1, keepdims=True)
            scores = jnp.where(masks[N_LEVELS], diag, jnp.where(masks[0], pair, 0.0))
            pieces = [scores[r:r + PACK_ROWS] for r in range(0, B_CHUNK, PACK_ROWS)]
            for lvl in range(1, N_LEVELS):
                half = 1 << lvl
                k_t = kst_ref[lvl - 1, hd, :, rs]
                compact = lvl > FULL_LEVELS
                if compact:
                    q_lvl = qc_ref[lvl - 1 - FULL_LEVELS, c * (B_CHUNK // 2):(c + 1) * (B_CHUNK // 2), sl]
                else:
                    q_lvl = qs_ref[lvl - 1, rs, sl]
                p = _dot(q_lvl, k_t)
                for i in range(len(pieces)):
                    r = i * PACK_ROWS
                    if compact and not r & half:
                        continue
                    src = (r // (2 * half)) * half + r % half if compact else r
                    pieces[i] = jnp.where(masks[lvl][r:r + PACK_ROWS], p[src:src + PACK_ROWS], pieces[i])
            scores = jnp.concatenate(pieces, axis=0)
            vh = v_ref[rs, sl]
            st = st_ref[hd]
            o = _dot(scores.astype(BF16), vh) + _dot(qe_ref[rs, sl], st.T.astype(BF16))
            ob_ref[rs, sl] = o
            st_ref[hd] = st * decay[:, sl] + _dot_tn(vh, ke_ref[rs, sl])

    zg = z_ref[:, base + 3 * D_B:base + 4 * D_B]
    gate = zg / (1.0 + jnp.exp2(zg * -LOG2E))
    y_ref[:, D_A:] = (_group_rms(ob_ref[...], nbo_ref[...], B_HEADS) * gate).astype(BF16)

    o_ref[...] = x + _dot(y_ref[...], wout_ref[...])


def _mlp_kernel(final, x_ref, nw_ref, wup_ref, wdn_ref, nf_ref, o_ref):
    x = x_ref[...]
    h = _rms(x, nw_ref[...]).astype(BF16)
    acc = x
    for j in range(D_FF // FF_BLOCK):
        cs = slice(j * FF_BLOCK, (j + 1) * FF_BLOCK)
        a = jnp.maximum(_dot(h, wup_ref[:, cs].astype(BF16)), 0.0)
        acc = acc + _dot((a * a).astype(BF16), wdn_ref[cs, :].astype(BF16))
    if final:
        acc = _rms(acc, nf_ref[...])
    o_ref[...] = acc


def _full(shape):
    return pl.BlockSpec(shape, lambda *_: (0,) * len(shape))


def _of_layer(shape, layer, **kwargs):
    return pl.BlockSpec((None,) + tuple(shape[1:]), lambda *_: (layer,) + (0,) * (len(shape) - 1),
                        **kwargs)


def _mixer(layer, x, nmix, win, sw, sbt, nv, nao, lbs, nbo, wout):
    bsz, seq, _ = x.shape
    rows = MIX_ROWS
    n_chunks = rows // B_CHUNK
    xspec = pl.BlockSpec((None, rows, D_MODEL), lambda bi, si: (bi, si, 0))
    return pl.pallas_call(
        functools.partial(_mixer_kernel, layer),
        grid=(bsz, seq // rows),
        in_specs=[xspec, _of_layer(nmix.shape, layer),
                  _of_layer(win.shape, layer, pipeline_mode=pl.Buffered(1)),
                  _of_layer(sw.shape, layer), _of_layer(sbt.shape, layer),
                  _of_layer(nv.shape, layer), _of_layer(nao.shape, layer), _full(lbs.shape),
                  _of_layer(nbo.shape, layer),
                  _of_layer(wout.shape, layer, pipeline_mode=pl.Buffered(1))],
        out_specs=xspec,
        out_shape=jax.ShapeDtypeStruct(x.shape, F32),
        scratch_shapes=[
            pltpu.VMEM((B_HEADS, HEAD, HEAD), F32),
            pltpu.VMEM((rows, D_IN), F32),
            pltpu.VMEM((rows, D_A), BF16),
            pltpu.VMEM((FULL_LEVELS, rows, D_B), BF16),
            pltpu.VMEM((N_LEVELS - 1 - FULL_LEVELS, rows // 2, D_B), BF16),
            pltpu.VMEM((N_LEVELS - 1, B_HEADS, HEAD, rows), BF16),
            pltpu.VMEM((rows, D_B), BF16),
            pltpu.VMEM((rows, D_B), BF16),
            pltpu.VMEM((rows, D_B), BF16),
            pltpu.VMEM((n_chunks, D_B), F32),
            pltpu.VMEM((rows, D_B), F32),
            pltpu.VMEM((rows, D_MODEL), BF16),
        ],
        compiler_params=pltpu.CompilerParams(
            dimension_semantics=("arbitrary", "arbitrary"), vmem_limit_bytes=VMEM_LIMIT_BYTES),
        name=f"mixer{layer}",
    )(x, nmix, win, sw, sbt, nv, nao, lbs, nbo, wout)


def _mlp(layer, final, x2, nw, w_up, w_down, nf):
    n_tok = x2.shape[0]
    xspec = pl.BlockSpec((MLP_ROWS, D_MODEL), lambda i: (i, 0))
    return pl.pallas_call(
        functools.partial(_mlp_kernel, final),
        grid=(n_tok // MLP_ROWS,),
        in_specs=[xspec, _of_layer(nw.shape, layer), _of_layer(w_up.shape, layer),
                  _of_layer(w_down.shape, layer), _full(nf.shape)],
        out_specs=xspec,
        out_shape=jax.ShapeDtypeStruct(x2.shape, F32),
        compiler_params=pltpu.CompilerParams(
            dimension_semantics=("arbitrary",), vmem_limit_bytes=VMEM_LIMIT_BYTES),
        name=f"mlp{layer}",
    )(x2, nw, w_up, w_down, nf)


def kernel(x, norm_mix, w_in, spatial_w, spatial_b, norm_v, norm_a_out, lower_bounds, norm_b_out,
           w_out, norm_mlp, w_up, w_down, norm_final):
    bsz, seq, d = x.shape
    depth = w_in.shape[0]
    assert d == D_MODEL and seq % MIX_ROWS == 0 and (bsz * seq) % MLP_ROWS == 0
    rows = lambda a: a.reshape(depth, 1, -1).astype(F32)
    nmix, nv, nao, nbo, nmlp = (rows(a) for a in (norm_mix, norm_v, norm_a_out, norm_b_out, norm_mlp))
    sbt = jnp.swapaxes(spatial_b, 1, 2)
    lbs = lower_bounds.astype(F32)
    wout = w_out.astype(BF16)
    nf = norm_final.reshape(1, -1).astype(F32)
    for l in range(depth):
        x = _mixer(l, x, nmix, w_in, spatial_w, sbt, nv, nao, lbs, nbo, wout)
        x2 = _mlp(l, l == depth - 1, x.reshape(bsz * seq, d), nmlp, w_up, w_down, nf)
        x = x2.reshape(bsz, seq, d)
    return x
```

```python
import functools

import jax
import jax.numpy as jnp
from jax import lax
from jax.experimental import pallas as pl
from jax.experimental.pallas import tpu as pltpu

D_MODEL = 1024
D_A = 512
D_B = 512
A_GROUPS = 4
A_CHUNK = 128
B_HEADS = 4
HEAD = 128
B_CHUNK = 128
D_IN = 2 * D_A + 4 * D_B
D_FF = 4 * D_MODEL
EPS = 1e-6
LOG2E = 1.4426950408889634
SQRT_2_OVER_PI = 0.7978845608028654

SUBLANES = 8
PACK_ROWS = 16
N_LEVELS = 7
FULL_LEVELS = PACK_ROWS.bit_length() - 2
VMEM_LIMIT_BYTES = 63 * 1024 * 1024

MIX_ROWS = 1024
IN_BLOCK = 512
MLP_ROWS = 1024
FF_BLOCK = 1024

BF16 = jnp.bfloat16
F32 = jnp.float32


def _dot(a, b):
    return jnp.dot(a, b, preferred_element_type=F32)


def _dot_tn(a, b):
    return lax.dot_general(a, b, (((0,), (0,)), ((), ())), preferred_element_type=F32)


def _rms(x, w):
    return x * lax.rsqrt(jnp.mean(x * x, axis=-1, keepdims=True) + EPS) * w


def _group_rms(x, w, groups):
    width = x.shape[-1] // groups
    parts = []
    for g in range(groups):
        sl = slice(g * width, (g + 1) * width)
        parts.append(_rms(x[:, sl], w[:, sl]))
    return jnp.concatenate(parts, axis=-1)


def _gelu(x):
    a = -2.0 * SQRT_2_OVER_PI * LOG2E
    return x / (1.0 + jnp.exp2(x * (a + (a * 0.044715) * (x * x))))


def _level_operands(q, k, b, half):
    rows, width = b.shape
    n = 2 * half
    if half >= SUBLANES:
        shape = (rows // n, 2, half, width)
        b4, q4, k4 = b.reshape(shape), q.reshape(shape), k.reshape(shape)
        first, second = b4[:, 0], b4[:, 1]
        mid = first[:, half - 1:half, :]
        zero = jnp.zeros_like(first)
        qe = q4[:, 1] * jnp.exp2(second - mid)
        ke = jnp.stack([k4[:, 0] * jnp.exp2(mid - first), zero], axis=1)
        if half >= PACK_ROWS:
            return qe.reshape(rows // 2, width), ke.reshape(rows, width)
        return jnp.stack([zero, qe], axis=1).reshape(rows, width), ke.reshape(rows, width)
    b3 = b.reshape(rows // SUBLANES, SUBLANES, width)
    pos = lax.broadcasted_iota(jnp.int32, b3.shape, 1) % n
    if half == SUBLANES // 2:
        mid = jnp.broadcast_to(b3[:, half - 1:half, :], b3.shape)
    else:
        assert half == 2
        prev1 = pltpu.roll(b3, 1, 1)
        nxt1 = pltpu.roll(b3, SUBLANES - 1, 1)
        prev2 = pltpu.roll(b3, 2, 1)
        mid = jnp.where(pos == 0, nxt1, jnp.where(pos == 1, b3, jnp.where(pos == 2, prev1, prev2)))
    sign = jnp.where(pos < half, -1.0, 1.0)
    e = jnp.exp2((b3 - mid) * sign).reshape(rows, width)
    return q * e, k * e


def _mixer_kernel(layer, x_ref, nmix_ref, win_ref, sw_ref, sbt_ref, nv_ref, nao_ref, lbs_ref,
                  nbo_ref, wout_ref, o_ref,
                  st_ref, z_ref, vn_ref, qs_ref, qc_ref, kst_ref, qe_ref, ke_ref, v_ref, eb_ref, ob_ref, y_ref):
    rows = x_ref.shape[0]
    lrow = slice(layer, layer + 1)

    @pl.when(pl.program_id(1) == 0)
    def _():
        st_ref[...] = jnp.zeros_like(st_ref)

    x = x_ref[...]
    h = _rms(x, nmix_ref[lrow, :])
    hb = h.astype(BF16)
    for j in range(D_IN // IN_BLOCK):
        cs = slice(j * IN_BLOCK, (j + 1) * IN_BLOCK)
        z_ref[:, cs] = _dot(hb, win_ref[:, cs].astype(BF16))

    u = _gelu(z_ref[:, 0:D_A])
    vn_ref[...] = _group_rms(_gelu(z_ref[:, D_A:2 * D_A]), nv_ref[lrow, :], A_GROUPS).astype(BF16)
    tri = (lax.broadcasted_iota(jnp.int32, (A_CHUNK, A_CHUNK), 0)
           >= lax.broadcasted_iota(jnp.int32, (A_CHUNK, A_CHUNK), 1))
    ya_parts = []
    for g in range(A_GROUPS):
        w_g = jnp.where(tri, sw_ref[g], 0.0).astype(BF16)
        bias_g = sbt_ref[:, g:g + 1]
        sl = slice(g * HEAD, (g + 1) * HEAD)
        blocks = []
        for c in range(rows // A_CHUNK):
            rs = slice(c * A_CHUNK, (c + 1) * A_CHUNK)
            mixed = _dot(w_g, vn_ref[rs, sl]) + bias_g
            blocks.append(u[rs, sl] * mixed)
        y_g = jnp.concatenate(blocks, axis=0)
        ya_parts.append(_rms(y_g, nao_ref[lrow, sl]))
    y_ref[:, 0:D_A] = jnp.concatenate(ya_parts, axis=-1).astype(BF16)

    base = 2 * D_A
    zq = z_ref[:, base:base + D_B]
    zf = z_ref[:, base + D_B:base + 2 * D_B]
    v_ref[...] = z_ref[:, base + 2 * D_B:base + 3 * D_B].astype(BF16)

    soft2 = jnp.log2(1.0 + jnp.exp2(-jnp.abs(zf) * LOG2E))
    log2_sig = jnp.minimum(zf, 0.0) * LOG2E - soft2
    if layer == 0:
        g2 = log2_sig
    else:
        lbs = lbs_ref[...]
        sm = jnp.exp(lbs - jnp.max(lbs, axis=0, keepdims=True))
        sm = sm / jnp.sum(sm, axis=0, keepdims=True)
        lb = jnp.sum(sm[1:layer + 1, :], axis=0, keepdims=True)
        t0 = jnp.log2(lb)
        t1 = jnp.log1p(-lb) * LOG2E + log2_sig
        g2 = jnp.maximum(t0, t1) + jnp.log2(1.0 + jnp.exp2(-jnp.abs(t0 - t1)))
    f = jnp.exp2(g2)
    kk = 1.0 - f

    g_hi = g2.astype(BF16)
    g_lo = (g2 - g_hi.astype(F32)).astype(BF16)
    ltri = (lax.broadcasted_iota(jnp.int32, (B_CHUNK, B_CHUNK), 0)
            >= lax.broadcasted_iota(jnp.int32, (B_CHUNK, B_CHUNK), 1)).astype(BF16)
    b_parts = []
    for c in range(rows // B_CHUNK):
        rs = slice(c * B_CHUNK, (c + 1) * B_CHUNK)
        b_parts.append(_dot(ltri, g_hi[rs]) + _dot(ltri, g_lo[rs]))
    b = jnp.concatenate(b_parts, axis=0)

    def put_kt(lvl, k_lvl):
        for hd in range(B_HEADS):
            kst_ref[lvl, hd] = k_lvl[:, hd * HEAD:(hd + 1) * HEAD].astype(BF16).T

    for lvl in range(1, N_LEVELS):
        q_lvl, k_lvl = _level_operands(zq, kk, b, 1 << lvl)
        if lvl <= FULL_LEVELS:
            qs_ref[lvl - 1] = q_lvl.astype(BF16)
        else:
            qc_ref[lvl - 1 - FULL_LEVELS] = q_lvl.astype(BF16)
        put_kt(lvl - 1, k_lvl)
    qf = zq * f

    b3 = b.reshape(rows // B_CHUNK, B_CHUNK, D_B)
    b_last = b3[:, B_CHUNK - 1:B_CHUNK, :]
    qe_ref[...] = (zq * jnp.exp2(b)).astype(BF16)
    ke_ref[...] = (kk.reshape(b3.shape) * jnp.exp2(b_last - b3)).reshape(rows, D_B).astype(BF16)
    eb_ref[...] = jnp.exp2(b_last).reshape(rows // B_CHUNK, D_B)

    ri = lax.broadcasted_iota(jnp.int32, (B_CHUNK, B_CHUNK), 0)
    ci = lax.broadcasted_iota(jnp.int32, (B_CHUNK, B_CHUNK), 1)
    xor = ri ^ ci
    masks = [(ri > ci) & (xor >= (1 << lvl)) & (xor < (2 << lvl)) for lvl in range(N_LEVELS)]
    masks.append(ri == ci)

    for c in range(rows // B_CHUNK):
        rs = slice(c * B_CHUNK, (c + 1) * B_CHUNK)
        decay = eb_ref[c:c + 1, :]
        for hd in range(B_HEADS):
            sl = slice(hd * HEAD, (hd + 1) * HEAD)
            kc = kk[rs, sl]
            k_prev = pltpu.roll(kc.reshape(B_CHUNK // SUBLANES, SUBLANES, HEAD), 1, 1)
            pair = jnp.sum(qf[rs, sl] * k_prev.reshape(B_CHUNK, HEAD), axis=-1, keepdims=True)
            diag = jnp.sum(zq[rs, sl] * kc, axis=-1, keepdims=True)
            scores = jnp.where(masks[N_LEVELS], diag, jnp.where(masks[0], pair, 0.0))
            pieces = [scores[r:r + PACK_ROWS] for r in range(0, B_CHUNK, PACK_ROWS)]
            for lvl in range(1, N_LEVELS):
                half = 1 << lvl
                k_t = kst_ref[lvl - 1, hd, :, rs]
                compact = lvl > FULL_LEVELS
                if compact:
                    q_lvl = qc_ref[lvl - 1 - FULL_LEVELS, c * (B_CHUNK // 2):(c + 1) * (B_CHUNK // 2), sl]
                else:
                    q_lvl = qs_ref[lvl - 1, rs, sl]
                p = _dot(q_lvl, k_t)
                for i in range(len(pieces)):
                    r = i * PACK_ROWS
                    if compact and not r & half:
                        continue
                    src = (r // (2 * half)) * half + r % half if compact else r
                    pieces[i] = jnp.where(masks[lvl][r:r + PACK_ROWS], p[src:src + PACK_ROWS], pieces[i])
            scores = jnp.concatenate(pieces, axis=0)
            vh = v_ref[rs, sl]
            st = st_ref[hd]
            o = _dot(scores.astype(BF16), vh) + _dot(qe_ref[rs, sl], st.T.astype(BF16))
            ob_ref[rs, sl] = o
            st_ref[hd] = st * decay[:, sl] + _dot_tn(vh, ke_ref[rs, sl])

    zg = z_ref[:, base + 3 * D_B:base + 4 * D_B]
    gate = zg / (1.0 + jnp.exp2(zg * -LOG2E))
    y_ref[:, D_A:] = (_group_rms(ob_ref[...], nbo_ref[lrow, :], B_HEADS) * gate).astype(BF16)

    o_ref[...] = x + _dot(y_ref[...], wout_ref[...])


def _mlp_kernel(layer, final, x_ref, nw_ref, wup_ref, wdn_ref, nf_ref, o_ref):
    x = x_ref[...]
    h = _rms(x, nw_ref[layer:layer + 1, :]).astype(BF16)
    acc = x
    for j in range(D_FF // FF_BLOCK):
        cs = slice(j * FF_BLOCK, (j + 1) * FF_BLOCK)
        a = jnp.maximum(_dot(h, wup_ref[:, cs].astype(BF16)), 0.0)
        acc = acc + _dot((a * a).astype(BF16), wdn_ref[cs, :].astype(BF16))
    if final:
        acc = _rms(acc, nf_ref[...])
    o_ref[...] = acc


def _full(shape):
    return pl.BlockSpec(shape, lambda *_: (0,) * len(shape))


def _of_layer(shape, layer, **kwargs):
    return pl.BlockSpec((None,) + tuple(shape[1:]), lambda *_: (layer,) + (0,) * (len(shape) - 1),
                        **kwargs)


def _mixer(layer, x, nmix, win, sw, sbt, nv, nao, lbs, nbo, wout):
    bsz, seq, _ = x.shape
    rows = MIX_ROWS
    n_chunks = rows // B_CHUNK
    xspec = pl.BlockSpec((None, rows, D_MODEL), lambda bi, si: (bi, si, 0))
    return pl.pallas_call(
        functools.partial(_mixer_kernel, layer),
        grid=(bsz, seq // rows),
        in_specs=[xspec, _full(nmix.shape),
                  _of_layer(win.shape, layer, pipeline_mode=pl.Buffered(1)),
                  _of_layer(sw.shape, layer), _of_layer(sbt.shape, layer),
                  _full(nv.shape), _full(nao.shape), _full(lbs.shape), _full(nbo.shape),
                  _of_layer(wout.shape, layer, pipeline_mode=pl.Buffered(1))],
        out_specs=xspec,
        out_shape=jax.ShapeDtypeStruct(x.shape, F32),
        scratch_shapes=[
            pltpu.VMEM((B_HEADS, HEAD, HEAD), F32),
            pltpu.VMEM((rows, D_IN), F32),
            pltpu.VMEM((rows, D_A), BF16),
            pltpu.VMEM((FULL_LEVELS, rows, D_B), BF16),
            pltpu.VMEM((N_LEVELS - 1 - FULL_LEVELS, rows // 2, D_B), BF16),
            pltpu.VMEM((N_LEVELS - 1, B_HEADS, HEAD, rows), BF16),
            pltpu.VMEM((rows, D_B), BF16),
            pltpu.VMEM((rows, D_B), BF16),
            pltpu.VMEM((rows, D_B), BF16),
            pltpu.VMEM((n_chunks, D_B), F32),
            pltpu.VMEM((rows, D_B), F32),
            pltpu.VMEM((rows, D_MODEL), BF16),
        ],
        compiler_params=pltpu.CompilerParams(
            dimension_semantics=("arbitrary", "arbitrary"), vmem_limit_bytes=VMEM_LIMIT_BYTES),
        name=f"mixer{layer}",
    )(x, nmix, win, sw, sbt, nv, nao, lbs, nbo, wout)


def _mlp(layer, final, x2, nw, w_up, w_down, nf):
    n_tok = x2.shape[0]
    xspec = pl.BlockSpec((MLP_ROWS, D_MODEL), lambda i: (i, 0))
    return pl.pallas_call(
        functools.partial(_mlp_kernel, layer, final),
        grid=(n_tok // MLP_ROWS,),
        in_specs=[xspec, _full(nw.shape), _of_layer(w_up.shape, layer),
                  _of_layer(w_down.shape, layer), _full(nf.shape)],
        out_specs=xspec,
        out_shape=jax.ShapeDtypeStruct(x2.shape, F32),
        compiler_params=pltpu.CompilerParams(
            dimension_semantics=("arbitrary",), vmem_limit_bytes=VMEM_LIMIT_BYTES),
        name=f"mlp{layer}",
    )(x2, nw, w_up, w_down, nf)


def kernel(x, norm_mix, w_in, spatial_w, spatial_b, norm_v, norm_a_out, lower_bounds, norm_b_out,
           w_out, norm_mlp, w_up, w_down, norm_final):
    bsz, seq, d = x.shape
    depth = w_in.shape[0]
    assert d == D_MODEL and seq % MIX_ROWS == 0 and (bsz * seq) % MLP_ROWS == 0
    nmix, nv, nao, nbo, nmlp = (a.astype(F32) for a in (norm_mix, norm_v, norm_a_out, norm_b_out, norm_mlp))
    sbt = jnp.swapaxes(spatial_b, 1, 2)
    lbs = lower_bounds.astype(F32)
    wout = w_out.astype(BF16)
    nf = norm_final.reshape(1, -1).astype(F32)
    for l in range(depth):
        x = _mixer(l, x, nmix, w_in, spatial_w, sbt, nv, nao, lbs, nbo, wout)
        x2 = _mlp(l, l == depth - 1, x.reshape(bsz * seq, d), nmlp, w_up, w_down, nf)
        x = x2.reshape(bsz, seq, d)
    return x
```

```python
import functools

import jax
import jax.numpy as jnp
from jax import lax
from jax.experimental import pallas as pl
from jax.experimental.pallas import tpu as pltpu

D_MODEL = 1024
D_A = 512
D_B = 512
A_GROUPS = 4
A_CHUNK = 128
B_HEADS = 4
HEAD = 128
B_CHUNK = 128
D_IN = 2 * D_A + 4 * D_B
D_FF = 4 * D_MODEL
EPS = 1e-6
LOG2E = 1.4426950408889634
SQRT_2_OVER_PI = 0.7978845608028654

SUBLANES = 8
PACK_ROWS = 16
N_LEVELS = 7
FULL_LEVELS = PACK_ROWS.bit_length() - 2
VMEM_LIMIT_BYTES = 63 * 1024 * 1024

MIX_ROWS = 1024
IN_BLOCK = 512
MLP_ROWS = 1024
FF_BLOCK = 1024

BF16 = jnp.bfloat16
F32 = jnp.float32


def _dot(a, b):
    return jnp.dot(a, b, preferred_element_type=F32)


def _dot_tn(a, b):
    return lax.dot_general(a, b, (((0,), (0,)), ((), ())), preferred_element_type=F32)


def _rms(x, w):
    return x * lax.rsqrt(jnp.mean(x * x, axis=-1, keepdims=True) + EPS) * w


def _group_rms(x, w, groups):
    width = x.shape[-1] // groups
    parts = []
    for g in range(groups):
        sl = slice(g * width, (g + 1) * width)
        parts.append(_rms(x[:, sl], w[:, sl]))
    return jnp.concatenate(parts, axis=-1)


def _gelu(x):
    a = -2.0 * SQRT_2_OVER_PI * LOG2E
    return x / (1.0 + jnp.exp2(x * (a + (a * 0.044715) * (x * x))))


def _level_operands(q, k, b, half):
    rows, width = b.shape
    n = 2 * half
    if half >= SUBLANES:
        shape = (rows // n, 2, half, width)
        b4, q4, k4 = b.reshape(shape), q.reshape(shape), k.reshape(shape)
        first, second = b4[:, 0], b4[:, 1]
        mid = first[:, half - 1:half, :]
        zero = jnp.zeros_like(first)
        qe = q4[:, 1] * jnp.exp2(second - mid)
        ke = jnp.stack([k4[:, 0] * jnp.exp2(mid - first), zero], axis=1)
        if half >= PACK_ROWS:
            return qe.reshape(rows // 2, width), ke.reshape(rows, width)
        return jnp.stack([zero, qe], axis=1).reshape(rows, width), ke.reshape(rows, width)
    b3 = b.reshape(rows // SUBLANES, SUBLANES, width)
    pos = lax.broadcasted_iota(jnp.int32, b3.shape, 1) % n
    if half == SUBLANES // 2:
        mid = jnp.broadcast_to(b3[:, half - 1:half, :], b3.shape)
    else:
        assert half == 2
        prev1 = pltpu.roll(b3, 1, 1)
        nxt1 = pltpu.roll(b3, SUBLANES - 1, 1)
        prev2 = pltpu.roll(b3, 2, 1)
        mid = jnp.where(pos == 0, nxt1, jnp.where(pos == 1, b3, jnp.where(pos == 2, prev1, prev2)))
    sign = jnp.where(pos < half, -1.0, 1.0)
    e = jnp.exp2((b3 - mid) * sign).reshape(rows, width)
    return q * e, k * e


def _mixer_kernel(layer, x_ref, nmix_ref, win_ref, sw_ref, sbt_ref, nv_ref, nao_ref, lbs_ref,
                  nbo_ref, wout_ref, o_ref,
                  st_ref, z_ref, vn_ref, qs_ref, qc_ref, kst_ref, qe_ref, ke_ref, v_ref, eb_ref, ob_ref, y_ref):
    rows = x_ref.shape[0]
    lrow = slice(layer, layer + 1)

    @pl.when(pl.program_id(1) == 0)
    def _():
        st_ref[...] = jnp.zeros_like(st_ref)

    x = x_ref[...]
    h = _rms(x, nmix_ref[lrow, :])
    hb = h.astype(BF16)
    for j in range(D_IN // IN_BLOCK):
        cs = slice(j * IN_BLOCK, (j + 1) * IN_BLOCK)
        z_ref[:, cs] = _dot(hb, win_ref[:, cs].astype(BF16))

    u = _gelu(z_ref[:, 0:D_A])
    vn_ref[...] = _group_rms(_gelu(z_ref[:, D_A:2 * D_A]), nv_ref[lrow, :], A_GROUPS).astype(BF16)
    tri = (lax.broadcasted_iota(jnp.int32, (A_CHUNK, A_CHUNK), 0)
           >= lax.broadcasted_iota(jnp.int32, (A_CHUNK, A_CHUNK), 1))
    ya_parts = []
    for g in range(A_GROUPS):
        w_g = jnp.where(tri, sw_ref[g], 0.0).astype(BF16)
        bias_g = sbt_ref[:, g:g + 1]
        sl = slice(g * HEAD, (g + 1) * HEAD)
        blocks = []
        for c in range(rows // A_CHUNK):
            rs = slice(c * A_CHUNK, (c + 1) * A_CHUNK)
            mixed = _dot(w_g, vn_ref[rs, sl]) + bias_g
            blocks.append(u[rs, sl] * mixed)
        y_g = jnp.concatenate(blocks, axis=0)
        ya_parts.append(_rms(y_g, nao_ref[lrow, sl]))
    y_ref[:, 0:D_A] = jnp.concatenate(ya_parts, axis=-1).astype(BF16)

    base = 2 * D_A
    zq = z_ref[:, base:base + D_B]
    zf = z_ref[:, base + D_B:base + 2 * D_B]
    v_ref[...] = z_ref[:, base + 2 * D_B:base + 3 * D_B].astype(BF16)

    soft2 = jnp.log2(1.0 + jnp.exp2(-jnp.abs(zf) * LOG2E))
    log2_sig = jnp.minimum(zf, 0.0) * LOG2E - soft2
    if layer == 0:
        g2 = log2_sig
    else:
        lbs = lbs_ref[...]
        sm = jnp.exp(lbs - jnp.max(lbs, axis=0, keepdims=True))
        sm = sm / jnp.sum(sm, axis=0, keepdims=True)
        lb = jnp.sum(sm[1:layer + 1, :], axis=0, keepdims=True)
        t0 = jnp.log2(lb)
        t1 = jnp.log1p(-lb) * LOG2E + log2_sig
        g2 = jnp.maximum(t0, t1) + jnp.log2(1.0 + jnp.exp2(-jnp.abs(t0 - t1)))
    f = jnp.exp2(g2)
    kk = 1.0 - f

    g_hi = g2.astype(BF16)
    g_lo = (g2 - g_hi.astype(F32)).astype(BF16)
    ltri = (lax.broadcasted_iota(jnp.int32, (B_CHUNK, B_CHUNK), 0)
            >= lax.broadcasted_iota(jnp.int32, (B_CHUNK, B_CHUNK), 1)).astype(BF16)
    b_parts = []
    for c in range(rows // B_CHUNK):
        rs = slice(c * B_CHUNK, (c + 1) * B_CHUNK)
        b_parts.append(_dot(ltri, g_hi[rs]) + _dot(ltri, g_lo[rs]))
    b = jnp.concatenate(b_parts, axis=0)

    def put_kt(lvl, k_lvl):
        for hd in range(B_HEADS):
            kst_ref[lvl, hd] = k_lvl[:, hd * HEAD:(hd + 1) * HEAD].astype(BF16).T

    for lvl in range(1, N_LEVELS):
        q_lvl, k_lvl = _level_operands(zq, kk, b, 1 << lvl)
        if lvl <= FULL_LEVELS:
            qs_ref[lvl - 1] = q_lvl.astype(BF16)
        else:
            qc_ref[lvl - 1 - FULL_LEVELS] = q_lvl.astype(BF16)
        put_kt(lvl - 1, k_lvl)
    qf = zq * f

    b3 = b.reshape(rows // B_CHUNK, B_CHUNK, D_B)
    b_last = b3[:, B_CHUNK - 1:B_CHUNK, :]
    qe_ref[...] = (zq * jnp.exp2(b)).astype(BF16)
    ke_ref[...] = (kk.reshape(b3.shape) * jnp.exp2(b_last - b3)).reshape(rows, D_B).astype(BF16)
    eb_ref[...] = jnp.exp2(b_last).reshape(rows // B_CHUNK, D_B)

    ri = lax.broadcasted_iota(jnp.int32, (B_CHUNK, B_CHUNK), 0)
    ci = lax.broadcasted_iota(jnp.int32, (B_CHUNK, B_CHUNK), 1)
    xor = ri ^ ci
    masks = [(ri > ci) & (xor >= (1 << lvl)) & (xor < (2 << lvl)) for lvl in range(N_LEVELS)]
    masks.append(ri == ci)

    for c in range(rows // B_CHUNK):
        rs = slice(c * B_CHUNK, (c + 1) * B_CHUNK)
        decay = eb_ref[c:c + 1, :]
        for hd in range(B_HEADS):
            sl = slice(hd * HEAD, (hd + 1) * HEAD)
            kc = kk[rs, sl]
            k_prev = pltpu.roll(kc.reshape(B_CHUNK // SUBLANES, SUBLANES, HEAD), 1, 1)
            pair = jnp.sum(qf[rs, sl] * k_prev.reshape(B_CHUNK, HEAD), axis=-1, keepdims=True)
            diag = jnp.sum(zq[rs, sl] * kc, axis=-1, keepdims=True)
            top = N_LEVELS - 1
            q_top = qc_ref[top - 1 - FULL_LEVELS, c * (B_CHUNK // 2):(c + 1) * (B_CHUNK // 2), sl]
            p_top = _dot(q_top, kst_ref[top - 1, hd, :, rs])
            scores = jnp.concatenate([jnp.zeros_like(p_top), p_top], axis=0)
            scores = jnp.where(masks[N_LEVELS], diag, jnp.where(masks[0], pair, scores))
            pieces = [scores[r:r + PACK_ROWS] for r in range(0, B_CHUNK, PACK_ROWS)]
            for lvl in range(1, top):
                half = 1 << lvl
                k_t = kst_ref[lvl - 1, hd, :, rs]
                compact = lvl > FULL_LEVELS
                if compact:
                    q_lvl = qc_ref[lvl - 1 - FULL_LEVELS, c * (B_CHUNK // 2):(c + 1) * (B_CHUNK // 2), sl]
                else:
                    q_lvl = qs_ref[lvl - 1, rs, sl]
                p = _dot(q_lvl, k_t)
                for i in range(len(pieces)):
                    r = i * PACK_ROWS
                    if compact and not r & half:
                        continue
                    src = (r // (2 * half)) * half + r % half if compact else r
                    pieces[i] = jnp.where(masks[lvl][r:r + PACK_ROWS], p[src:src + PACK_ROWS], pieces[i])
            scores = jnp.concatenate(pieces, axis=0)
            vh = v_ref[rs, sl]
            st = st_ref[hd]
            o = _dot(scores.astype(BF16), vh) + _dot(qe_ref[rs, sl], st.T.astype(BF16))
            ob_ref[rs, sl] = o
            st_ref[hd] = st * decay[:, sl] + _dot_tn(vh, ke_ref[rs, sl])

    zg = z_ref[:, base + 3 * D_B:base + 4 * D_B]
    gate = zg / (1.0 + jnp.exp2(zg * -LOG2E))
    y_ref[:, D_A:] = (_group_rms(ob_ref[...], nbo_ref[lrow, :], B_HEADS) * gate).astype(BF16)

    o_ref[...] = x + _dot(y_ref[...], wout_ref[...])


def _mlp_kernel(layer, final, x_ref, nw_ref, wup_ref, wdn_ref, nf_ref, o_ref):
    x = x_ref[...]
    h = _rms(x, nw_ref[layer:layer + 1, :]).astype(BF16)
    acc = x
    for j in range(D_FF // FF_BLOCK):
        cs = slice(j * FF_BLOCK, (j + 1) * FF_BLOCK)
        a = jnp.maximum(_dot(h, wup_ref[:, cs].astype(BF16)), 0.0)
        acc = acc + _dot((a * a).astype(BF16), wdn_ref[cs, :].astype(BF16))
    if final:
        acc = _rms(acc, nf_ref[...])
    o_ref[...] = acc


def _full(shape):
    return pl.BlockSpec(shape, lambda *_: (0,) * len(shape))


def _of_layer(shape, layer, **kwargs):
    return pl.BlockSpec((None,) + tuple(shape[1:]), lambda *_: (layer,) + (0,) * (len(shape) - 1),
                        **kwargs)


def _mixer(layer, x, nmix, win, sw, sbt, nv, nao, lbs, nbo, wout):
    bsz, seq, _ = x.shape
    rows = MIX_ROWS
    n_chunks = rows // B_CHUNK
    xspec = pl.BlockSpec((None, rows, D_MODEL), lambda bi, si: (bi, si, 0))
    return pl.pallas_call(
        functools.partial(_mixer_kernel, layer),
        grid=(bsz, seq // rows),
        in_specs=[xspec, _full(nmix.shape),
                  _of_layer(win.shape, layer, pipeline_mode=pl.Buffered(1)),
                  _of_layer(sw.shape, layer), _of_layer(sbt.shape, layer),
                  _full(nv.shape), _full(nao.shape), _full(lbs.shape), _full(nbo.shape),
                  _of_layer(wout.shape, layer, pipeline_mode=pl.Buffered(1))],
        out_specs=xspec,
        out_shape=jax.ShapeDtypeStruct(x.shape, F32),
        scratch_shapes=[
            pltpu.VMEM((B_HEADS, HEAD, HEAD), F32),
            pltpu.VMEM((rows, D_IN), F32),
            pltpu.VMEM((rows, D_A), BF16),
            pltpu.VMEM((FULL_LEVELS, rows, D_B), BF16),
            pltpu.VMEM((N_LEVELS - 1 - FULL_LEVELS, rows // 2, D_B), BF16),
            pltpu.VMEM((N_LEVELS - 1, B_HEADS, HEAD, rows), BF16),
            pltpu.VMEM((rows, D_B), BF16),
            pltpu.VMEM((rows, D_B), BF16),
            pltpu.VMEM((rows, D_B), BF16),
            pltpu.VMEM((n_chunks, D_B), F32),
            pltpu.VMEM((rows, D_B), F32),
            pltpu.VMEM((rows, D_MODEL), BF16),
        ],
        compiler_params=pltpu.CompilerParams(
            dimension_semantics=("arbitrary", "arbitrary"), vmem_limit_bytes=VMEM_LIMIT_BYTES),
        name=f"mixer{layer}",
    )(x, nmix, win, sw, sbt, nv, nao, lbs, nbo, wout)


def _mlp(layer, final, x2, nw, w_up, w_down, nf):
    n_tok = x2.shape[0]
    xspec = pl.BlockSpec((MLP_ROWS, D_MODEL), lambda i: (i, 0))
    return pl.pallas_call(
        functools.partial(_mlp_kernel, layer, final),
        grid=(n_tok // MLP_ROWS,),
        in_specs=[xspec, _full(nw.shape), _of_layer(w_up.shape, layer),
                  _of_layer(w_down.shape, layer), _full(nf.shape)],
        out_specs=xspec,
        out_shape=jax.ShapeDtypeStruct(x2.shape, F32),
        compiler_params=pltpu.CompilerParams(
            dimension_semantics=("arbitrary",), vmem_limit_bytes=VMEM_LIMIT_BYTES),
        name=f"mlp{layer}",
    )(x2, nw, w_up, w_down, nf)


def kernel(x, norm_mix, w_in, spatial_w, spatial_b, norm_v, norm_a_out, lower_bounds, norm_b_out,
           w_out, norm_mlp, w_up, w_down, norm_final):
    bsz, seq, d = x.shape
    depth = w_in.shape[0]
    assert d == D_MODEL and seq % MIX_ROWS == 0 and (bsz * seq) % MLP_ROWS == 0
    nmix, nv, nao, nbo, nmlp = (a.astype(F32) for a in (norm_mix, norm_v, norm_a_out, norm_b_out, norm_mlp))
    sbt = jnp.swapaxes(spatial_b, 1, 2)
    lbs = lower_bounds.astype(F32)
    wout = w_out.astype(BF16)
    nf = norm_final.reshape(1, -1).astype(F32)
    for l in range(depth):
        x = _mixer(l, x, nmix, w_in, spatial_w, sbt, nv, nao, lbs, nbo, wout)
        x2 = _mlp(l, l == depth - 1, x.reshape(bsz * seq, d), nmlp, w_up, w_down, nf)
        x = x2.reshape(bsz, seq, d)
    return x
```

```python
import functools

import jax
import jax.numpy as jnp
from jax import lax
from jax.experimental import pallas as pl
from jax.experimental.pallas import tpu as pltpu

D_MODEL = 1024
D_A = 512
D_B = 512
A_GROUPS = 4
A_CHUNK = 128
B_HEADS = 4
HEAD = 128
B_CHUNK = 128
D_IN = 2 * D_A + 4 * D_B
D_FF = 4 * D_MODEL
EPS = 1e-6
LOG2E = 1.4426950408889634
SQRT_2_OVER_PI = 0.7978845608028654

SUBLANES = 8
PACK_ROWS = 16
N_LEVELS = 7
FULL_LEVELS = PACK_ROWS.bit_length() - 2
VMEM_LIMIT_BYTES = 63 * 1024 * 1024

MIX_ROWS = 1024
IN_BLOCK = 512
MLP_ROWS = 1024
FF_BLOCK = 1024

BF16 = jnp.bfloat16
F32 = jnp.float32


def _dot(a, b):
    return jnp.dot(a, b, preferred_element_type=F32)


def _dot_tn(a, b):
    return lax.dot_general(a, b, (((0,), (0,)), ((), ())), preferred_element_type=F32)


def _rms(x, w):
    return x * lax.rsqrt(jnp.mean(x * x, axis=-1, keepdims=True) + EPS) * w


def _group_rms(x, w, groups):
    width = x.shape[-1] // groups
    parts = []
    for g in range(groups):
        sl = slice(g * width, (g + 1) * width)
        parts.append(_rms(x[:, sl], w[:, sl]))
    return jnp.concatenate(parts, axis=-1)


def _gelu(x):
    a = -2.0 * SQRT_2_OVER_PI * LOG2E
    return x / (1.0 + jnp.exp2(x * (a + (a * 0.044715) * (x * x))))


def _level_operands(q, k, b, half):
    rows, width = b.shape
    n = 2 * half
    if half >= SUBLANES:
        shape = (rows // n, 2, half, width)
        b4, q4, k4 = b.reshape(shape), q.reshape(shape), k.reshape(shape)
        first, second = b4[:, 0], b4[:, 1]
        mid = first[:, half - 1:half, :]
        zero = jnp.zeros_like(first)
        qe = q4[:, 1] * jnp.exp2(second - mid)
        ke = jnp.stack([k4[:, 0] * jnp.exp2(mid - first), zero], axis=1)
        if half >= PACK_ROWS:
            return qe.reshape(rows // 2, width), ke.reshape(rows, width)
        return jnp.stack([zero, qe], axis=1).reshape(rows, width), ke.reshape(rows, width)
    b3 = b.reshape(rows // SUBLANES, SUBLANES, width)
    pos = lax.broadcasted_iota(jnp.int32, b3.shape, 1) % n
    if half == SUBLANES // 2:
        mid = jnp.broadcast_to(b3[:, half - 1:half, :], b3.shape)
    else:
        assert half == 2
        prev1 = pltpu.roll(b3, 1, 1)
        nxt1 = pltpu.roll(b3, SUBLANES - 1, 1)
        prev2 = pltpu.roll(b3, 2, 1)
        mid = jnp.where(pos == 0, nxt1, jnp.where(pos == 1, b3, jnp.where(pos == 2, prev1, prev2)))
    sign = jnp.where(pos < half, -1.0, 1.0)
    e = jnp.exp2((b3 - mid) * sign).reshape(rows, width)
    return q * e, k * e


def _mixer_kernel(layer, x_ref, nmix_ref, win_ref, sw_ref, sbt_ref, nv_ref, nao_ref, lbs_ref,
                  nbo_ref, wout_ref, o_ref,
                  st_ref, z_ref, vn_ref, qs_ref, qc_ref, kst_ref, qe_ref, ke_ref, v_ref, eb_ref, ob_ref, y_ref):
    rows = x_ref.shape[0]
    lrow = slice(layer, layer + 1)

    @pl.when(pl.program_id(1) == 0)
    def _():
        st_ref[...] = jnp.zeros_like(st_ref)

    x = x_ref[...]
    h = _rms(x, nmix_ref[lrow, :])
    hb = h.astype(BF16)
    for j in range(D_IN // IN_BLOCK):
        cs = slice(j * IN_BLOCK, (j + 1) * IN_BLOCK)
        z_ref[:, cs] = _dot(hb, win_ref[:, cs].astype(BF16))

    u = _gelu(z_ref[:, 0:D_A])
    vn_ref[...] = _group_rms(_gelu(z_ref[:, D_A:2 * D_A]), nv_ref[lrow, :], A_GROUPS).astype(BF16)
    tri = (lax.broadcasted_iota(jnp.int32, (A_CHUNK, A_CHUNK), 0)
           >= lax.broadcasted_iota(jnp.int32, (A_CHUNK, A_CHUNK), 1))
    ya_parts = []
    for g in range(A_GROUPS):
        w_g = jnp.where(tri, sw_ref[g], 0.0).astype(BF16)
        bias_g = sbt_ref[:, g:g + 1]
        sl = slice(g * HEAD, (g + 1) * HEAD)
        blocks = []
        for c in range(rows // A_CHUNK):
            rs = slice(c * A_CHUNK, (c + 1) * A_CHUNK)
            mixed = _dot(w_g, vn_ref[rs, sl]) + bias_g
            blocks.append(u[rs, sl] * mixed)
        y_g = jnp.concatenate(blocks, axis=0)
        ya_parts.append(_rms(y_g, nao_ref[lrow, sl]))
    y_ref[:, 0:D_A] = jnp.concatenate(ya_parts, axis=-1).astype(BF16)

    base = 2 * D_A
    v_ref[...] = z_ref[:, base + 2 * D_B:base + 3 * D_B].astype(BF16)
    if layer > 0:
        lbs = lbs_ref[...]
        sm = jnp.exp(lbs - jnp.max(lbs, axis=0, keepdims=True))
        sm = sm / jnp.sum(sm, axis=0, keepdims=True)
        lb = jnp.sum(sm[1:layer + 1, :], axis=0, keepdims=True)
        t0_all = jnp.log2(lb)
        l1m_all = jnp.log1p(-lb) * LOG2E
    ltri = (lax.broadcasted_iota(jnp.int32, (B_CHUNK, B_CHUNK), 0)
            >= lax.broadcasted_iota(jnp.int32, (B_CHUNK, B_CHUNK), 1)).astype(BF16)

    zq_parts, kk_parts, qf_parts = [], [], []
    for hd in range(B_HEADS):
        hs = slice(hd * HEAD, (hd + 1) * HEAD)
        zq = z_ref[:, base + hd * HEAD:base + (hd + 1) * HEAD]
        zf = z_ref[:, base + D_B + hd * HEAD:base + D_B + (hd + 1) * HEAD]
        soft2 = jnp.log2(1.0 + jnp.exp2(-jnp.abs(zf) * LOG2E))
        log2_sig = jnp.minimum(zf, 0.0) * LOG2E - soft2
        if layer == 0:
            g2 = log2_sig
        else:
            t0 = t0_all[:, hs]
            t1 = l1m_all[:, hs] + log2_sig
            g2 = jnp.maximum(t0, t1) + jnp.log2(1.0 + jnp.exp2(-jnp.abs(t0 - t1)))
        f = jnp.exp2(g2)
        kk = 1.0 - f

        g_hi = g2.astype(BF16)
        g_lo = (g2 - g_hi.astype(F32)).astype(BF16)
        b_parts = []
        for c in range(rows // B_CHUNK):
            rs = slice(c * B_CHUNK, (c + 1) * B_CHUNK)
            b_parts.append(_dot(ltri, g_hi[rs]) + _dot(ltri, g_lo[rs]))
        b = jnp.concatenate(b_parts, axis=0)

        for lvl in range(1, N_LEVELS):
            q_lvl, k_lvl = _level_operands(zq, kk, b, 1 << lvl)
            if lvl <= FULL_LEVELS:
                qs_ref[lvl - 1, :, hs] = q_lvl.astype(BF16)
            else:
                qc_ref[lvl - 1 - FULL_LEVELS, :, hs] = q_lvl.astype(BF16)
            kst_ref[lvl - 1, hd] = k_lvl.astype(BF16).T

        b3 = b.reshape(rows // B_CHUNK, B_CHUNK, HEAD)
        b_last = b3[:, B_CHUNK - 1:B_CHUNK, :]
        qe_ref[:, hs] = (zq * jnp.exp2(b)).astype(BF16)
        ke_ref[:, hs] = (kk.reshape(b3.shape) * jnp.exp2(b_last - b3)).reshape(rows, HEAD).astype(BF16)
        eb_ref[:, hs] = jnp.exp2(b_last).reshape(rows // B_CHUNK, HEAD)
        zq_parts.append(zq)
        kk_parts.append(kk)
        qf_parts.append(zq * f)
    zq = jnp.concatenate(zq_parts, axis=-1)
    kk = jnp.concatenate(kk_parts, axis=-1)
    qf = jnp.concatenate(qf_parts, axis=-1)

    ri = lax.broadcasted_iota(jnp.int32, (B_CHUNK, B_CHUNK), 0)
    ci = lax.broadcasted_iota(jnp.int32, (B_CHUNK, B_CHUNK), 1)
    xor = ri ^ ci
    masks = [(ri > ci) & (xor >= (1 << lvl)) & (xor < (2 << lvl)) for lvl in range(N_LEVELS)]
    masks.append(ri == ci)

    for c in range(rows // B_CHUNK):
        rs = slice(c * B_CHUNK, (c + 1) * B_CHUNK)
        decay = eb_ref[c:c + 1, :]
        for hd in range(B_HEADS):
            sl = slice(hd * HEAD, (hd + 1) * HEAD)
            kc = kk[rs, sl]
            k_prev = pltpu.roll(kc.reshape(B_CHUNK // SUBLANES, SUBLANES, HEAD), 1, 1)
            pair = jnp.sum(qf[rs, sl] * k_prev.reshape(B_CHUNK, HEAD), axis=-1, keepdims=True)
            diag = jnp.sum(zq[rs, sl] * kc, axis=-1, keepdims=True)
            top = N_LEVELS - 1
            q_top = qc_ref[top - 1 - FULL_LEVELS, c * (B_CHUNK // 2):(c + 1) * (B_CHUNK // 2), sl]
            p_top = _dot(q_top, kst_ref[top - 1, hd, :, rs])
            scores = jnp.concatenate([jnp.zeros_like(p_top), p_top], axis=0)
            scores = jnp.where(masks[N_LEVELS], diag, jnp.where(masks[0], pair, scores))
            pieces = [scores[r:r + PACK_ROWS] for r in range(0, B_CHUNK, PACK_ROWS)]
            for lvl in range(1, top):
                half = 1 << lvl
                k_t = kst_ref[lvl - 1, hd, :, rs]
                compact = lvl > FULL_LEVELS
                if compact:
                    q_lvl = qc_ref[lvl - 1 - FULL_LEVELS, c * (B_CHUNK // 2):(c + 1) * (B_CHUNK // 2), sl]
                else:
                    q_lvl = qs_ref[lvl - 1, rs, sl]
                p = _dot(q_lvl, k_t)
                for i in range(len(pieces)):
                    r = i * PACK_ROWS
                    if compact and not r & half:
                        continue
                    src = (r // (2 * half)) * half + r % half if compact else r
                    pieces[i] = jnp.where(masks[lvl][r:r + PACK_ROWS], p[src:src + PACK_ROWS], pieces[i])
            scores = jnp.concatenate(pieces, axis=0)
            vh = v_ref[rs, sl]
            st = st_ref[hd]
            o = _dot(scores.astype(BF16), vh) + _dot(qe_ref[rs, sl], st.T.astype(BF16))
            ob_ref[rs, sl] = o
            st_ref[hd] = st * decay[:, sl] + _dot_tn(vh, ke_ref[rs, sl])

    zg = z_ref[:, base + 3 * D_B:base + 4 * D_B]
    gate = zg / (1.0 + jnp.exp2(zg * -LOG2E))
    y_ref[:, D_A:] = (_group_rms(ob_ref[...], nbo_ref[lrow, :], B_HEADS) * gate).astype(BF16)

    o_ref[...] = x + _dot(y_ref[...], wout_ref[...])


def _mlp_kernel(layer, final, x_ref, nw_ref, wup_ref, wdn_ref, nf_ref, o_ref):
    x = x_ref[...]
    h = _rms(x, nw_ref[layer:layer + 1, :]).astype(BF16)
    acc = x
    for j in range(D_FF // FF_BLOCK):
        cs = slice(j * FF_BLOCK, (j + 1) * FF_BLOCK)
        a = jnp.maximum(_dot(h, wup_ref[:, cs].astype(BF16)), 0.0)
        acc = acc + _dot((a * a).astype(BF16), wdn_ref[cs, :].astype(BF16))
    if final:
        acc = _rms(acc, nf_ref[...])
    o_ref[...] = acc


def _full(shape):
    return pl.BlockSpec(shape, lambda *_: (0,) * len(shape))


def _of_layer(shape, layer, **kwargs):
    return pl.BlockSpec((None,) + tuple(shape[1:]), lambda *_: (layer,) + (0,) * (len(shape) - 1),
                        **kwargs)


def _mixer(layer, x, nmix, win, sw, sbt, nv, nao, lbs, nbo, wout):
    bsz, seq, _ = x.shape
    rows = MIX_ROWS
    n_chunks = rows // B_CHUNK
    xspec = pl.BlockSpec((None, rows, D_MODEL), lambda bi, si: (bi, si, 0))
    return pl.pallas_call(
        functools.partial(_mixer_kernel, layer),
        grid=(bsz, seq // rows),
        in_specs=[xspec, _full(nmix.shape),
                  _of_layer(win.shape, layer, pipeline_mode=pl.Buffered(1)),
                  _of_layer(sw.shape, layer), _of_layer(sbt.shape, layer),
                  _full(nv.shape), _full(nao.shape), _full(lbs.shape), _full(nbo.shape),
                  _of_layer(wout.shape, layer, pipeline_mode=pl.Buffered(1))],
        out_specs=xspec,
        out_shape=jax.ShapeDtypeStruct(x.shape, F32),
        scratch_shapes=[
            pltpu.VMEM((B_HEADS, HEAD, HEAD), F32),
            pltpu.VMEM((rows, D_IN), F32),
            pltpu.VMEM((rows, D_A), BF16),
            pltpu.VMEM((FULL_LEVELS, rows, D_B), BF16),
            pltpu.VMEM((N_LEVELS - 1 - FULL_LEVELS, rows // 2, D_B), BF16),
            pltpu.VMEM((N_LEVELS - 1, B_HEADS, HEAD, rows), BF16),
            pltpu.VMEM((rows, D_B), BF16),
            pltpu.VMEM((rows, D_B), BF16),
            pltpu.VMEM((rows, D_B), BF16),
            pltpu.VMEM((n_chunks, D_B), F32),
            pltpu.VMEM((rows, D_B), F32),
            pltpu.VMEM((rows, D_MODEL), BF16),
        ],
        compiler_params=pltpu.CompilerParams(
            dimension_semantics=("arbitrary", "arbitrary"), vmem_limit_bytes=VMEM_LIMIT_BYTES),
        name=f"mixer{layer}",
    )(x, nmix, win, sw, sbt, nv, nao, lbs, nbo, wout)


def _mlp(layer, final, x2, nw, w_up, w_down, nf):
    n_tok = x2.shape[0]
    xspec = pl.BlockSpec((MLP_ROWS, D_MODEL), lambda i: (i, 0))
    return pl.pallas_call(
        functools.partial(_mlp_kernel, layer, final),
        grid=(n_tok // MLP_ROWS,),
        in_specs=[xspec, _full(nw.shape), _of_layer(w_up.shape, layer),
                  _of_layer(w_down.shape, layer), _full(nf.shape)],
        out_specs=xspec,
        out_shape=jax.ShapeDtypeStruct(x2.shape, F32),
        compiler_params=pltpu.CompilerParams(
            dimension_semantics=("arbitrary",), vmem_limit_bytes=VMEM_LIMIT_BYTES),
        name=f"mlp{layer}",
    )(x2, nw, w_up, w_down, nf)


def kernel(x, norm_mix, w_in, spatial_w, spatial_b, norm_v, norm_a_out, lower_bounds, norm_b_out,
           w_out, norm_mlp, w_up, w_down, norm_final):
    bsz, seq, d = x.shape
    depth = w_in.shape[0]
    assert d == D_MODEL and seq % MIX_ROWS == 0 and (bsz * seq) % MLP_ROWS == 0
    nmix, nv, nao, nbo, nmlp = (a.astype(F32) for a in (norm_mix, norm_v, norm_a_out, norm_b_out, norm_mlp))
    sbt = jnp.swapaxes(spatial_b, 1, 2)
    lbs = lower_bounds.astype(F32)
    wout = w_out.astype(BF16)
    nf = norm_final.reshape(1, -1).astype(F32)
    for l in range(depth):
        x = _mixer(l, x, nmix, w_in, spatial_w, sbt, nv, nao, lbs, nbo, wout)
        x2 = _mlp(l, l == depth - 1, x.reshape(bsz * seq, d), nmlp, w_up, w_down, nf)
        x = x2.reshape(bsz, seq, d)
    return x
```

```python
import functools

import jax
import jax.numpy as jnp
from jax import lax
from jax.experimental import pallas as pl
from jax.experimental.pallas import tpu as pltpu

D_MODEL = 1024
D_A = 512
D_B = 512
A_GROUPS = 4
A_CHUNK = 128
B_HEADS = 4
HEAD = 128
B_CHUNK = 128
D_IN = 2 * D_A + 4 * D_B
D_FF = 4 * D_MODEL
EPS = 1e-6
LOG2E = 1.4426950408889634
SQRT_2_OVER_PI = 0.7978845608028654

SUBLANES = 8
PACK_ROWS = 16
N_LEVELS = 7
FULL_LEVELS = PACK_ROWS.bit_length() - 2
VMEM_LIMIT_BYTES = 63 * 1024 * 1024

MIX_ROWS = 1024
IN_BLOCK = 512
MLP_ROWS = 1024
FF_BLOCK = 1024

BF16 = jnp.bfloat16
F32 = jnp.float32


def _dot(a, b):
    return jnp.dot(a, b, preferred_element_type=F32)


def _dot_tn(a, b):
    return lax.dot_general(a, b, (((0,), (0,)), ((), ())), preferred_element_type=F32)


def _rms(x, w):
    return x * lax.rsqrt(jnp.mean(x * x, axis=-1, keepdims=True) + EPS) * w


def _group_rms(x, w, groups):
    width = x.shape[-1] // groups
    parts = []
    for g in range(groups):
        sl = slice(g * width, (g + 1) * width)
        parts.append(_rms(x[:, sl], w[:, sl]))
    return jnp.concatenate(parts, axis=-1)


def _gelu(x):
    a = -2.0 * SQRT_2_OVER_PI * LOG2E
    return x / (1.0 + jnp.exp2(x * (a + (a * 0.044715) * (x * x))))


def _level_operands(q, k, b, half):
    rows, width = b.shape
    n = 2 * half
    if half >= SUBLANES:
        shape = (rows // n, 2, half, width)
        b4, q4, k4 = b.reshape(shape), q.reshape(shape), k.reshape(shape)
        first, second = b4[:, 0], b4[:, 1]
        mid = first[:, half - 1:half, :]
        zero = jnp.zeros_like(first)
        qe = q4[:, 1] * jnp.exp2(second - mid)
        ke = jnp.stack([k4[:, 0] * jnp.exp2(mid - first), zero], axis=1)
        if half >= PACK_ROWS:
            return qe.reshape(rows // 2, width), ke.reshape(rows, width)
        return jnp.stack([zero, qe], axis=1).reshape(rows, width), ke.reshape(rows, width)
    b3 = b.reshape(rows // SUBLANES, SUBLANES, width)
    pos = lax.broadcasted_iota(jnp.int32, b3.shape, 1) % n
    if half == SUBLANES // 2:
        mid = jnp.broadcast_to(b3[:, half - 1:half, :], b3.shape)
    else:
        assert half == 2
        prev1 = pltpu.roll(b3, 1, 1)
        nxt1 = pltpu.roll(b3, SUBLANES - 1, 1)
        prev2 = pltpu.roll(b3, 2, 1)
        mid = jnp.where(pos == 0, nxt1, jnp.where(pos == 1, b3, jnp.where(pos == 2, prev1, prev2)))
    sign = jnp.where(pos < half, -1.0, 1.0)
    e = jnp.exp2((b3 - mid) * sign).reshape(rows, width)
    return q * e, k * e


def _mixer_kernel(layer, x_ref, nmix_ref, win_ref, sw_ref, sbt_ref, nv_ref, nao_ref, lbs_ref,
                  nbo_ref, wout_ref, o_ref,
                  st_ref, z_ref, vn_ref, qs_ref, qc_ref, kst_ref, qe_ref, ke_ref, v_ref, eb_ref, ob_ref, y_ref):
    rows = x_ref.shape[0]
    lrow = slice(layer, layer + 1)

    @pl.when(pl.program_id(1) == 0)
    def _():
        st_ref[...] = jnp.zeros_like(st_ref)

    x = x_ref[...]
    h = _rms(x, nmix_ref[lrow, :])
    hb = h.astype(BF16)
    for j in range(D_IN // IN_BLOCK):
        cs = slice(j * IN_BLOCK, (j + 1) * IN_BLOCK)
        z_ref[:, cs] = _dot(hb, win_ref[:, cs].astype(BF16))

    u = _gelu(z_ref[:, 0:D_A])
    vn_ref[...] = _group_rms(_gelu(z_ref[:, D_A:2 * D_A]), nv_ref[lrow, :], A_GROUPS).astype(BF16)
    tri = (lax.broadcasted_iota(jnp.int32, (A_CHUNK, A_CHUNK), 0)
           >= lax.broadcasted_iota(jnp.int32, (A_CHUNK, A_CHUNK), 1))
    ya_parts = []
    for g in range(A_GROUPS):
        w_g = jnp.where(tri, sw_ref[g], 0.0).astype(BF16)
        bias_g = sbt_ref[:, g:g + 1]
        sl = slice(g * HEAD, (g + 1) * HEAD)
        blocks = []
        for c in range(rows // A_CHUNK):
            rs = slice(c * A_CHUNK, (c + 1) * A_CHUNK)
            mixed = _dot(w_g, vn_ref[rs, sl]) + bias_g
            blocks.append(u[rs, sl] * mixed)
        y_g = jnp.concatenate(blocks, axis=0)
        ya_parts.append(_rms(y_g, nao_ref[lrow, sl]))
    y_ref[:, 0:D_A] = jnp.concatenate(ya_parts, axis=-1).astype(BF16)

    base = 2 * D_A
    zq = z_ref[:, base:base + D_B]
    zf = z_ref[:, base + D_B:base + 2 * D_B]
    v_ref[...] = z_ref[:, base + 2 * D_B:base + 3 * D_B].astype(BF16)

    soft2 = jnp.log2(1.0 + jnp.exp2(-jnp.abs(zf) * LOG2E))
    log2_sig = jnp.minimum(zf, 0.0) * LOG2E - soft2
    if layer == 0:
        g2 = log2_sig
    else:
        lbs = lbs_ref[...]
        sm = jnp.exp(lbs - jnp.max(lbs, axis=0, keepdims=True))
        sm = sm / jnp.sum(sm, axis=0, keepdims=True)
        lb = jnp.sum(sm[1:layer + 1, :], axis=0, keepdims=True)
        t0 = jnp.log2(lb)
        t1 = jnp.log1p(-lb) * LOG2E + log2_sig
        g2 = jnp.maximum(t0, t1) + jnp.log2(1.0 + jnp.exp2(-jnp.abs(t0 - t1)))
    f = jnp.exp2(g2)
    kk = 1.0 - f

    g_hi = g2.astype(BF16)
    g_lo = (g2 - g_hi.astype(F32)).astype(BF16)
    ltri = (lax.broadcasted_iota(jnp.int32, (B_CHUNK, B_CHUNK), 0)
            >= lax.broadcasted_iota(jnp.int32, (B_CHUNK, B_CHUNK), 1)).astype(BF16)
    b_parts = []
    for c in range(rows // B_CHUNK):
        rs = slice(c * B_CHUNK, (c + 1) * B_CHUNK)
        b_parts.append(_dot(ltri, g_hi[rs]) + _dot(ltri, g_lo[rs]))
    b = jnp.concatenate(b_parts, axis=0)

    def put_kt(lvl, k_lvl):
        for hd in range(B_HEADS):
            kst_ref[lvl, hd] = k_lvl[:, hd * HEAD:(hd + 1) * HEAD].astype(BF16).T

    for lvl in range(1, N_LEVELS):
        q_lvl, k_lvl = _level_operands(zq, kk, b, 1 << lvl)
        if lvl <= FULL_LEVELS:
            qs_ref[lvl - 1] = q_lvl.astype(BF16)
        else:
            qc_ref[lvl - 1 - FULL_LEVELS] = q_lvl.astype(BF16)
        put_kt(lvl - 1, k_lvl)
    qf = zq * f

    b3 = b.reshape(rows // B_CHUNK, B_CHUNK, D_B)
    b_last = b3[:, B_CHUNK - 1:B_CHUNK, :]
    qe_ref[...] = (zq * jnp.exp2(b)).astype(BF16)
    ke_ref[...] = (kk.reshape(b3.shape) * jnp.exp2(b_last - b3)).reshape(rows, D_B).astype(BF16)
    eb_ref[...] = jnp.exp2(b_last).reshape(rows // B_CHUNK, D_B)

    ri = lax.broadcasted_iota(jnp.int32, (B_CHUNK, B_CHUNK), 0)
    ci = lax.broadcasted_iota(jnp.int32, (B_CHUNK, B_CHUNK), 1)
    xor = ri ^ ci
    masks = [(ri > ci) & (xor >= (1 << lvl)) & (xor < (2 << lvl)) for lvl in range(N_LEVELS)]
    masks.append(ri == ci)

    for c in range(rows // B_CHUNK):
        rs = slice(c * B_CHUNK, (c + 1) * B_CHUNK)
        decay = eb_ref[c:c + 1, :]
        for hd in range(B_HEADS):
            sl = slice(hd * HEAD, (hd + 1) * HEAD)
            kc = kk[rs, sl]
            k_prev = pltpu.roll(kc.reshape(B_CHUNK // SUBLANES, SUBLANES, HEAD), 1, 1)
            pair = jnp.sum(qf[rs, sl] * k_prev.reshape(B_CHUNK, HEAD), axis=-1, keepdims=True)
            diag = jnp.sum(zq[rs, sl] * kc, axis=-1, keepdims=True)
            top = N_LEVELS - 1
            q_top = qc_ref[top - 1 - FULL_LEVELS, c * (B_CHUNK // 2):(c + 1) * (B_CHUNK // 2), sl]
            p_top = _dot(q_top, kst_ref[top - 1, hd, :, rs])
            scores = jnp.concatenate([jnp.zeros_like(p_top), p_top], axis=0)
            scores = jnp.where(masks[N_LEVELS], diag, jnp.where(masks[0], pair, scores))
            pieces = [scores[r:r + PACK_ROWS] for r in range(0, B_CHUNK, PACK_ROWS)]
            for lvl in range(1, top):
                half = 1 << lvl
                k_t = kst_ref[lvl - 1, hd, :, rs]
                compact = lvl > FULL_LEVELS
                if compact:
                    q_lvl = qc_ref[lvl - 1 - FULL_LEVELS, c * (B_CHUNK // 2):(c + 1) * (B_CHUNK // 2), sl]
                else:
                    q_lvl = qs_ref[lvl - 1, rs, sl]
                p = _dot(q_lvl, k_t)
                for i in range(len(pieces)):
                    r = i * PACK_ROWS
                    if compact and not r & half:
                        continue
                    src = (r // (2 * half)) * half + r % half if compact else r
                    pieces[i] = jnp.where(masks[lvl][r:r + PACK_ROWS], p[src:src + PACK_ROWS], pieces[i])
            scores = jnp.concatenate(pieces, axis=0)
            vh = v_ref[rs, sl]
            st = st_ref[hd]
            o = _dot(scores.astype(BF16), vh) + _dot(qe_ref[rs, sl], st.astype(BF16).T)
            ob_ref[rs, sl] = o
            st_ref[hd] = st * decay[:, sl] + _dot_tn(vh, ke_ref[rs, sl])

    zg = z_ref[:, base + 3 * D_B:base + 4 * D_B]
    gate = zg / (1.0 + jnp.exp2(zg * -LOG2E))
    y_ref[:, D_A:] = (_group_rms(ob_ref[...], nbo_ref[lrow, :], B_HEADS) * gate).astype(BF16)

    o_ref[...] = x + _dot(y_ref[...], wout_ref[...])


def _mlp_kernel(layer, final, x_ref, nw_ref, wup_ref, wdn_ref, nf_ref, o_ref):
    x = x_ref[...]
    h = _rms(x, nw_ref[layer:layer + 1, :]).astype(BF16)
    acc = x
    for j in range(D_FF // FF_BLOCK):
        cs = slice(j * FF_BLOCK, (j + 1) * FF_BLOCK)
        a = jnp.maximum(_dot(h, wup_ref[:, cs].astype(BF16)), 0.0)
        acc = acc + _dot((a * a).astype(BF16), wdn_ref[cs, :].astype(BF16))
    if final:
        acc = _rms(acc, nf_ref[...])
    o_ref[...] = acc


def _full(shape):
    return pl.BlockSpec(shape, lambda *_: (0,) * len(shape))


def _of_layer(shape, layer, **kwargs):
    return pl.BlockSpec((None,) + tuple(shape[1:]), lambda *_: (layer,) + (0,) * (len(shape) - 1),
                        **kwargs)


def _mixer(layer, x, nmix, win, sw, sbt, nv, nao, lbs, nbo, wout):
    bsz, seq, _ = x.shape
    rows = MIX_ROWS
    n_chunks = rows // B_CHUNK
    xspec = pl.BlockSpec((None, rows, D_MODEL), lambda bi, si: (bi, si, 0))
    return pl.pallas_call(
        functools.partial(_mixer_kernel, layer),
        grid=(bsz, seq // rows),
        in_specs=[xspec, _full(nmix.shape),
                  _of_layer(win.shape, layer, pipeline_mode=pl.Buffered(1)),
                  _of_layer(sw.shape, layer), _of_layer(sbt.shape, layer),
                  _full(nv.shape), _full(nao.shape), _full(lbs.shape), _full(nbo.shape),
                  _of_layer(wout.shape, layer, pipeline_mode=pl.Buffered(1))],
        out_specs=xspec,
        out_shape=jax.ShapeDtypeStruct(x.shape, F32),
        scratch_shapes=[
            pltpu.VMEM((B_HEADS, HEAD, HEAD), F32),
            pltpu.VMEM((rows, D_IN), F32),
            pltpu.VMEM((rows, D_A), BF16),
            pltpu.VMEM((FULL_LEVELS, rows, D_B), BF16),
            pltpu.VMEM((N_LEVELS - 1 - FULL_LEVELS, rows // 2, D_B), BF16),
            pltpu.VMEM((N_LEVELS - 1, B_HEADS, HEAD, rows), BF16),
            pltpu.VMEM((rows, D_B), BF16),
            pltpu.VMEM((rows, D_B), BF16),
            pltpu.VMEM((rows, D_B), BF16),
            pltpu.VMEM((n_chunks, D_B), F32),
            pltpu.VMEM((rows, D_B), F32),
            pltpu.VMEM((rows, D_MODEL), BF16),
        ],
        compiler_params=pltpu.CompilerParams(
            dimension_semantics=("arbitrary", "arbitrary"), vmem_limit_bytes=VMEM_LIMIT_BYTES),
        name=f"mixer{layer}",
    )(x, nmix, win, sw, sbt, nv, nao, lbs, nbo, wout)


def _mlp(layer, final, x2, nw, w_up, w_down, nf):
    n_tok = x2.shape[0]
    xspec = pl.BlockSpec((MLP_ROWS, D_MODEL), lambda i: (i, 0))
    return pl.pallas_call(
        functools.partial(_mlp_kernel, layer, final),
        grid=(n_tok // MLP_ROWS,),
        in_specs=[xspec, _full(nw.shape), _of_layer(w_up.shape, layer),
                  _of_layer(w_down.shape, layer), _full(nf.shape)],
        out_specs=xspec,
        out_shape=jax.ShapeDtypeStruct(x2.shape, F32),
        compiler_params=pltpu.CompilerParams(
            dimension_semantics=("arbitrary",), vmem_limit_bytes=VMEM_LIMIT_BYTES),
        name=f"mlp{layer}",
    )(x2, nw, w_up, w_down, nf)


def kernel(x, norm_mix, w_in, spatial_w, spatial_b, norm_v, norm_a_out, lower_bounds, norm_b_out,
           w_out, norm_mlp, w_up, w_down, norm_final):
    bsz, seq, d = x.shape
    depth = w_in.shape[0]
    assert d == D_MODEL and seq % MIX_ROWS == 0 and (bsz * seq) % MLP_ROWS == 0
    nmix, nv, nao, nbo, nmlp = (a.astype(F32) for a in (norm_mix, norm_v, norm_a_out, norm_b_out, norm_mlp))
    sbt = jnp.swapaxes(spatial_b, 1, 2)
    lbs = lower_bounds.astype(F32)
    wout = w_out.astype(BF16)
    nf = norm_final.reshape(1, -1).astype(F32)
    for l in range(depth):
        x = _mixer(l, x, nmix, w_in, spatial_w, sbt, nv, nao, lbs, nbo, wout)
        x2 = _mlp(l, l == depth - 1, x.reshape(bsz * seq, d), nmlp, w_up, w_down, nf)
        x = x2.reshape(bsz, seq, d)
    return x
```
